```python
import jax, jax.numpy as jnp
from jax import lax
import numpy as np

D_MODEL = 2048
BATCH = 4
SEQ = 2048
DEPTH = 2
DEC_BATCH = 128
DEC_SEQ = 1
PAST_LEN = 16384
PAGE_SIZE = 128

N_MIXERS = 2
N_LRU_LAYERS = (DEPTH + 1) // 2
N_POOL_LAYERS = DEPTH // 2
LRU_BLOCK = 128
D_RNN = int(round(4 * D_MODEL / 3 / LRU_BLOCK)) * LRU_BLOCK
LRU_HEADS = D_RNN // LRU_BLOCK
CONV_WIDTH = 4
LRU_C = 8.0
POOL_WINDOWS = (2, 4, 8, 16)
POOL_GROUPS = len(POOL_WINDOWS)
POOL_GROUP_DIM = D_MODEL // POOL_GROUPS
POOL_BUF = max(POOL_WINDOWS) - 1
N_EXPERTS = 64
TOP_K = 6
N_GROUPS = 8
TOPK_GROUPS = 4
D_EXPERT = (D_MODEL * 11) // 16
ROUTED_SCALE = 2.5
MOE_BLOCK = 128
EPS = 1e-6

kernel_name = "hybrid_rglru_pool_moe_adaln_step"

F32 = jnp.float32


def rms_norm(x, g):
    x32 = x.astype(F32)
    y = x32 * lax.rsqrt(jnp.mean(x32 * x32, axis=-1, keepdims=True) + EPS)
    return (y * g.astype(F32)).astype(x.dtype)


def modulate(x, g, shift, scale):
    return rms_norm(x, g) * (1 + scale[:, None, :]) + shift[:, None, :]


def rglru_mixer(u, h0, conv0, w_in, conv_w, conv_b, w_rg, b_rg, w_ig, b_ig, lam, w_out):
    B, L, _ = u.shape
    xb = u @ w_in
    gate = jax.nn.gelu(xb[..., :D_RNN])
    rec = xb[..., D_RNN:]
    ext = jnp.concatenate([conv0.astype(rec.dtype), rec], axis=1)
    xc = conv_b + ext[:, 0:L] * conv_w[0]
    for k in range(1, CONV_WIDTH):
        xc = xc + ext[:, k:k + L] * conv_w[k]
    new_conv = ext[:, L:]
    xh = xc.reshape(B, L, LRU_HEADS, LRU_BLOCK)
    r = jax.nn.sigmoid(jnp.einsum('blhi,hij->blhj', xh, w_rg).reshape(B, L, D_RNN) + b_rg)
    ig = jax.nn.sigmoid(jnp.einsum('blhi,hij->blhj', xh, w_ig).reshape(B, L, D_RNN) + b_ig)
    log_a = -LRU_C * r.astype(F32) * jax.nn.softplus(-lam.astype(F32))
    a = jnp.exp(log_a)
    b = jnp.sqrt(-jnp.expm1(2.0 * log_a)) * (ig * xc).astype(F32)
    b = b.at[:, 0].add(a[:, 0] * h0.astype(F32))

    def combine(left, right):
        a1, b1 = left
        a2, b2 = right
        return a1 * a2, a2 * b1 + b2

    _, h = lax.associative_scan(combine, (a, b), axis=1)
    y = (gate * h.astype(u.dtype)) @ w_out
    return y, h[:, -1], new_conv


def pool_mixer(u, buf0, pos0, w_pool, scale):
    B, L, D = u.shape
    ext = jnp.concatenate([buf0.astype(u.dtype), u], axis=1)
    ext32 = ext.astype(F32)
    cs = jnp.concatenate([jnp.zeros((B, 1, D), F32), jnp.cumsum(ext32, axis=1)], axis=1)
    hi = cs[:, POOL_BUF + 1:POOL_BUF + 1 + L]
    pos = pos0 + jnp.arange(L)
    u32 = u.astype(F32)
    outs = []
    for g, w in enumerate(POOL_WINDOWS):
        sl = slice(g * POOL_GROUP_DIM, (g + 1) * POOL_GROUP_DIM)
        lo = cs[:, POOL_BUF + 1 - w:POOL_BUF + 1 - w + L, sl]
        cnt = jnp.minimum(w, pos + 1).astype(F32)[None, :, None]
        outs.append((hi[..., sl] - lo) / cnt - u32[..., sl])
    p = jnp.stack(outs, axis=2).astype(u.dtype)
    y = jnp.einsum('blgi,gij->blgj', p, w_pool).reshape(B, L, D) * scale
    return y, ext[:, L:]


def moe(u, router_w, router_bias, w_gate, w_up, w_down, ws_gate, ws_up, ws_down):
    B, L, D = u.shape
    T = B * L
    xt = u.reshape(T, D)
    scores = jax.nn.sigmoid(xt.astype(F32) @ router_w.astype(F32))
    biased = scores + router_bias.astype(F32)
    grp = lax.top_k(biased.reshape(T, N_GROUPS, N_EXPERTS // N_GROUPS), 2)[0].sum(-1)
    _, gidx = lax.top_k(grp, TOPK_GROUPS)
    gmask = jax.nn.one_hot(gidx, N_GROUPS, dtype=F32).sum(1)
    emask = jnp.repeat(gmask, N_EXPERTS // N_GROUPS, axis=1)
    _, eidx = lax.top_k(jnp.where(emask > 0, biased, -jnp.inf), TOP_K)
    wsel = jnp.take_along_axis(scores, eidx, axis=1)
    wsel = wsel / jnp.sum(wsel, axis=-1, keepdims=True) * ROUTED_SCALE

    TK = T * TOP_K
    e_flat = eidx.reshape(-1)
    tok_flat = jnp.arange(TK) // TOP_K
    w_flat = wsel.reshape(-1)
    order = jnp.argsort(e_flat)
    e_sorted = e_flat[order]
    counts = jnp.bincount(e_flat, length=N_EXPERTS)
    starts = jnp.cumsum(counts) - counts
    padded = ((counts + MOE_BLOCK - 1) // MOE_BLOCK) * MOE_BLOCK
    pends = jnp.cumsum(padded)
    pstarts = pends - padded
    dest = pstarts[e_sorted] + (jnp.arange(TK) - starts[e_sorted])
    NB = (TK + N_EXPERTS * (MOE_BLOCK - 1)) // MOE_BLOCK
    R = NB * MOE_BLOCK
    row_tok = jnp.full((R,), T, jnp.int32).at[dest].set(tok_flat[order].astype(jnp.int32))
    row_w = jnp.zeros((R,), F32).at[dest].set(w_flat[order])
    block_e = jnp.clip(jnp.searchsorted(pends, jnp.arange(NB) * MOE_BLOCK, side='right'), 0, N_EXPERTS - 1)
    xs = jnp.concatenate([xt, jnp.zeros((1, D), xt.dtype)], axis=0)[row_tok].reshape(NB, MOE_BLOCK, D)

    def expert_block(args):
        xblk, e = args
        return (jax.nn.silu(xblk @ w_gate[e]) * (xblk @ w_up[e])) @ w_down[e]

    yb = lax.map(expert_block, (xs, block_e)).reshape(R, D)
    routed = jax.ops.segment_sum(yb.astype(F32) * row_w[:, None], row_tok, num_segments=T + 1)[:T]
    shared = (jax.nn.silu(xt @ ws_gate) * (xt @ ws_up)) @ ws_down
    return (routed.astype(u.dtype) + shared).reshape(B, L, D)


def trunk(x, c, pos0, lru_h, lru_conv, pool_buf, p):
    new_h, new_conv, new_pool = [], [], []
    for i in range(DEPTH):
        mod = jax.nn.silu(c) @ p['ada_w'][i] + p['ada_b'][i]
        sh1, sc1, g1, sh2, sc2, g2 = jnp.split(mod, 6, axis=-1)
        u = modulate(x, p['norm_mix'][i], sh1, sc1)
        j = i // N_MIXERS
        if i % N_MIXERS == 0:
            y, h, cv = rglru_mixer(u, lru_h[j], lru_conv[j], p['lru_w_in'][j], p['lru_conv_w'][j],
                                   p['lru_conv_b'][j], p['lru_w_rg'][j], p['lru_b_rg'][j],
                                   p['lru_w_ig'][j], p['lru_b_ig'][j], p['lru_lambda'][j], p['lru_w_out'][j])
            new_h.append(h)
            new_conv.append(cv)
        else:
            y, pb = pool_mixer(u, pool_buf[j], pos0, p['pool_w'][j], p['pool_scale'][j])
            new_pool.append(pb)
        x = x + g1[:, None, :] * y
        u = modulate(x, p['norm_ffn'][i], sh2, sc2)
        x = x + g2[:, None, :] * moe(u, p['router_w'][i], p['router_bias'][i], p['exp_w_gate'][i],
                                     p['exp_w_up'][i], p['exp_w_down'][i], p['shared_w_gate'][i],
                                     p['shared_w_up'][i], p['shared_w_down'][i])
    x = rms_norm(x, p['norm_out'])
    return x, jnp.stack(new_h), jnp.stack(new_conv), jnp.stack(new_pool)


def setup_inputs(seed: int = 0) -> dict:
    key = jax.random.key(seed)
    ks = jax.random.split(key, 40)
    nrm = jax.random.normal
    d, dr, f, e = D_MODEL, D_RNN, D_EXPERT, N_EXPERTS
    ua = jax.random.uniform(ks[0], (N_LRU_LAYERS, dr), F32, 0.9, 0.999) ** (1.0 / LRU_C)
    return {
        "x_prompt": nrm(ks[1], (BATCH, SEQ, d), F32),
        "x_sample": nrm(ks[2], (DEC_BATCH, DEC_SEQ, d), F32),
        "state_lru_h": 0.5 * nrm(ks[3], (N_LRU_LAYERS, DEC_BATCH, dr), F32),
        "state_lru_conv": nrm(ks[4], (N_LRU_LAYERS, DEC_BATCH, CONV_WIDTH - 1, dr), F32),
        "state_pool": nrm(ks[5], (N_POOL_LAYERS, DEC_BATCH, POOL_BUF, d), F32),
        "c_prompt": nrm(ks[6], (BATCH, d), F32),
        "c_sample": nrm(ks[7], (DEC_BATCH, d), F32),
        "ada_w": 0.5 * d ** -0.5 * nrm(ks[8], (DEPTH, d, 6 * d), F32),
        "ada_b": 0.02 * nrm(ks[9], (DEPTH, 6 * d), F32),
        "norm_mix": 1.0 + 0.05 * nrm(ks[10], (DEPTH, d), F32),
        "norm_ffn": 1.0 + 0.05 * nrm(ks[11], (DEPTH, d), F32),
        "norm_out": 1.0 + 0.05 * nrm(ks[12], (d,), F32),
        "lru_w_in": d ** -0.5 * nrm(ks[13], (N_LRU_LAYERS, d, 2 * dr), F32),
        "lru_conv_w": 0.5 * nrm(ks[14], (N_LRU_LAYERS, CONV_WIDTH, dr), F32),
        "lru_conv_b": 0.02 * nrm(ks[15], (N_LRU_LAYERS, dr), F32),
        "lru_w_rg": LRU_BLOCK ** -0.5 * nrm(ks[16], (N_LRU_LAYERS, LRU_HEADS, LRU_BLOCK, LRU_BLOCK), F32),
        "lru_b_rg": 0.02 * nrm(ks[17], (N_LRU_LAYERS, dr), F32),
        "lru_w_ig": LRU_BLOCK ** -0.5 * nrm(ks[18], (N_LRU_LAYERS, LRU_HEADS, LRU_BLOCK, LRU_BLOCK), F32),
        "lru_b_ig": 0.02 * nrm(ks[19], (N_LRU_LAYERS, dr), F32),
        "lru_lambda": jnp.log(ua) - jnp.log1p(-ua),
        "lru_w_out": dr ** -0.5 * nrm(ks[20], (N_LRU_LAYERS, dr, d), F32),
        "pool_w": POOL_GROUP_DIM ** -0.5 * nrm(ks[21], (N_POOL_LAYERS, POOL_GROUPS, POOL_GROUP_DIM, POOL_GROUP_DIM), F32),
        "pool_scale": 1.0 + 0.1 * nrm(ks[22], (N_POOL_LAYERS, d), F32),
        "router_w": d ** -0.5 * nrm(ks[23], (DEPTH, d, e), F32),
        "router_bias": 0.01 * nrm(ks[24], (DEPTH, e), F32),
        "exp_w_gate": d ** -0.5 * nrm(ks[25], (DEPTH, e, d, f), F32),
        "exp_w_up": d ** -0.5 * nrm(ks[26], (DEPTH, e, d, f), F32),
        "exp_w_down": f ** -0.5 * nrm(ks[27], (DEPTH, e, f, d), F32),
        "shared_w_gate": d ** -0.5 * nrm(ks[28], (DEPTH, d, f), F32),
        "shared_w_up": d ** -0.5 * nrm(ks[29], (DEPTH, d, f), F32),
        "shared_w_down": f ** -0.5 * nrm(ks[30], (DEPTH, f, d), F32),
    }


def reference(x_prompt, x_sample, state_lru_h, state_lru_conv, state_pool, c_prompt, c_sample,
              ada_w, ada_b, norm_mix, norm_ffn, norm_out, lru_w_in, lru_conv_w, lru_conv_b,
              lru_w_rg, lru_b_rg, lru_w_ig, lru_b_ig, lru_lambda, lru_w_out, pool_w, pool_scale,
              router_w, router_bias, exp_w_gate, exp_w_up, exp_w_down,
              shared_w_gate, shared_w_up, shared_w_down):
    p = dict(ada_w=ada_w, ada_b=ada_b, norm_mix=norm_mix, norm_ffn=norm_ffn, norm_out=norm_out,
             lru_w_in=lru_w_in, lru_conv_w=lru_conv_w, lru_conv_b=lru_conv_b, lru_w_rg=lru_w_rg,
             lru_b_rg=lru_b_rg, lru_w_ig=lru_w_ig, lru_b_ig=lru_b_ig, lru_lambda=lru_lambda,
             lru_w_out=lru_w_out, pool_w=pool_w, pool_scale=pool_scale, router_w=router_w,
             router_bias=router_bias, exp_w_gate=exp_w_gate, exp_w_up=exp_w_up, exp_w_down=exp_w_down,
             shared_w_gate=shared_w_gate, shared_w_up=shared_w_up, shared_w_down=shared_w_down)
    bp = x_prompt.shape[0]
    h0 = jnp.zeros((N_LRU_LAYERS, bp, D_RNN), x_prompt.dtype)
    cv0 = jnp.zeros((N_LRU_LAYERS, bp, CONV_WIDTH - 1, D_RNN), x_prompt.dtype)
    pb0 = jnp.zeros((N_POOL_LAYERS, bp, POOL_BUF, D_MODEL), x_prompt.dtype)
    y_prompt, h_p, cv_p, pb_p = trunk(x_prompt, c_prompt, 0, h0, cv0, pb0, p)
    y_sample, h_s, cv_s, pb_s = trunk(x_sample, c_sample, PAST_LEN, state_lru_h, state_lru_conv, state_pool, p)
    return (y_prompt, y_sample, h_p, cv_p, pb_p, h_s, cv_s, pb_s)
```

```python
import functools

import jax
import jax.numpy as jnp
from jax import lax
from jax.experimental import pallas as pl
from jax.experimental.pallas import tpu as pltpu

F32 = jnp.float32
BF16 = jnp.bfloat16
I32 = jnp.int32
U32 = jnp.uint32

EPS = 1e-6
LRU_C = 8.0
LRU_BLOCK = 128
POOL_WINDOWS = (2, 4, 8, 16)
POOL_BUF = max(POOL_WINDOWS) - 1
TOP_K = 6
N_GROUPS = 8
TOPK_GROUPS = 4
ROUTED_SCALE = 2.5
PAST_LEN = 16384

V7X_VMEM_LIMIT_BYTES = 58 * 1024 * 1024
LANE = 128


def _cparams(n_axes):
    return pltpu.CompilerParams(dimension_semantics=("arbitrary",) * n_axes,
                                vmem_limit_bytes=V7X_VMEM_LIMIT_BYTES)


def _pick(n, candidates):
    for c in candidates:
        if c <= n and n % c == 0:
            return c
    return n


def _bdot(a, b):
    return jnp.dot(a.astype(BF16), b.astype(BF16), preferred_element_type=F32)


def _silu(x):
    return x * jax.nn.sigmoid(x)


def _ada_kernel(c_ref, w_ref, b_ref, o_ref):
    s = _silu(c_ref[...])
    o_ref[0] = _bdot(s, w_ref[0]) + b_ref[0]


def _ada(c_all, ada_w, ada_b):
    depth, d, n = ada_w.shape
    bc = c_all.shape[0]
    tn = _pick(n, (1024, 512, 256, 128))
    return pl.pallas_call(
        _ada_kernel,
        grid=(depth, n // tn),
        in_specs=[pl.BlockSpec((bc, d), lambda l, j: (0, 0)),
                  pl.BlockSpec((1, d, tn), lambda l, j: (l, 0, j)),
                  pl.BlockSpec((1, 1, tn), lambda l, j: (l, 0, j))],
        out_specs=pl.BlockSpec((1, bc, tn), lambda l, j: (l, 0, j)),
        out_shape=jax.ShapeDtypeStruct((depth, bc, n), F32),
        compiler_params=_cparams(2),
        name="ada_mod",
    )(c_all, ada_w, ada_b.reshape(depth, 1, n))


def _rms(x, g):
    return x * lax.rsqrt(jnp.mean(x * x, axis=-1, keepdims=True) + EPS) * g


def _modnorm(x, g, shift, scale):
    return _rms(x, g) * (1.0 + scale) + shift


def _mod_spec(mod, tm):
    d = mod.shape[-1]
    if mod.shape[1] == 1:
        return pl.BlockSpec((1, 1, d), lambda b, i: (b, 0, 0))
    return pl.BlockSpec((1, tm, d), lambda b, i: (b, i, 0))


def _norm_mix_kernel(x_ref, g_ref, sh_ref, sc_ref, o_ref):
    o_ref[0] = _modnorm(x_ref[0], g_ref[...], sh_ref[0], sc_ref[0]).astype(o_ref.dtype)


def _norm_mix(x, g, shift, scale):
    bx, l, d = x.shape
    tm = _pick(l, (512, 256, 128, 64, 32, 16))
    return pl.pallas_call(
        _norm_mix_kernel,
        grid=(bx, l // tm),
        in_specs=[pl.BlockSpec((1, tm, d), lambda b, i: (b, i, 0)),
                  pl.BlockSpec((1, d), lambda b, i: (0, 0)),
                  _mod_spec(shift, tm), _mod_spec(scale, tm)],
        out_specs=pl.BlockSpec((1, tm, d), lambda b, i: (b, i, 0)),
        out_shape=jax.ShapeDtypeStruct((bx, l, d), BF16),
        compiler_params=_cparams(2),
        name="norm_mix",
    )(x, g.reshape(1, d), shift, scale)


def _norm_out_kernel(x_ref, g_ref, o_ref):
    o_ref[0] = _rms(x_ref[0], g_ref[...])


def _norm_out(x, g):
    bx, l, d = x.shape
    tm = _pick(l, (512, 256, 128, 64, 32, 16, 8))
    return pl.pallas_call(
        _norm_out_kernel,
        grid=(bx, l // tm),
        in_specs=[pl.BlockSpec((1, tm, d), lambda b, i: (b, i, 0)),
                  pl.BlockSpec((1, d), lambda b, i: (0, 0))],
        out_specs=pl.BlockSpec((1, tm, d), lambda b, i: (b, i, 0)),
        out_shape=jax.ShapeDtypeStruct((bx, l, d), F32),
        compiler_params=_cparams(2),
        name="norm_out",
    )(x, g.reshape(1, d))


def _pack_pair(lo, hi):
    lo_b = lax.bitcast_convert_type(lo.astype(BF16).astype(F32), U32)
    hi_b = lax.bitcast_convert_type(hi.astype(BF16).astype(F32), U32)
    return lax.shift_right_logical(lo_b, jnp.uint32(16)) | (hi_b & jnp.uint32(0xFFFF0000))


def _unpack_pair(word):
    lo = lax.bitcast_convert_type(lax.shift_left(word, jnp.uint32(16)), F32)
    hi = lax.bitcast_convert_type(word & jnp.uint32(0xFFFF0000), F32)
    return lo, hi


def _norm_ffn_kernel(*refs, aliased):
    if aliased:
        refs = refs[1:]
    x_ref, g_ref, sh_ref, sc_ref, rwt_ref, up_ref, lg_ref = refs
    u = _modnorm(x_ref[0], g_ref[...], sh_ref[0], sc_ref[0])
    half = u.shape[1] // 2
    up_ref[...] = _pack_pair(u[:, :half], u[:, half:])
    lg_ref[...] = lax.dot_general(rwt_ref[...], u, (((1,), (1,)), ((), ())),
                                  precision=lax.Precision.HIGHEST, preferred_element_type=F32)


def _norm_ffn(x, g, shift, scale, router_wt, row0, t_all, packed_buf=None):
    bx, l, d = x.shape
    e = router_wt.shape[0]
    tm = _pick(l, (512, 256, 128))
    nl = l // tm
    blk0 = row0 // tm
    aliased = packed_buf is not None
    in_specs = [pl.BlockSpec((1, tm, d), lambda b, i: (b, i, 0)),
                pl.BlockSpec((1, d), lambda b, i: (0, 0)),
                _mod_spec(shift, tm), _mod_spec(scale, tm),
                pl.BlockSpec((e, d), lambda b, i: (0, 0))]
    args = [x, g.reshape(1, d), shift, scale, router_wt]
    if aliased:
        in_specs = [pl.BlockSpec(memory_space=pl.ANY)] + in_specs
        args = [packed_buf] + args
    return pl.pallas_call(
        functools.partial(_norm_ffn_kernel, aliased=aliased),
        grid=(bx, nl),
        in_specs=in_specs,
        out_specs=[pl.BlockSpec((tm, d // 2), lambda b, i: (blk0 + b * nl + i, 0)),
                   pl.BlockSpec((e, tm), lambda b, i: (0, b * nl + i))],
        out_shape=[jax.ShapeDtypeStruct((t_all, d // 2), U32),
                   jax.ShapeDtypeStruct((e, bx * l), F32)],
        input_output_aliases={0: 0} if aliased else {},
        compiler_params=_cparams(2),
        name="norm_ffn",
    )(*args)


def _mm_kernel(*refs, act, resid):
    if resid:
        x_ref, w_ref, r_ref, g_ref, o_ref, wbf = refs
    else:
        x_ref, w_ref, o_ref, wbf = refs

    @pl.when((pl.program_id(1) == 0) & (pl.program_id(2) == 0))
    def _():
        wbf[...] = w_ref[...].astype(BF16)

    acc = jnp.dot(x_ref[0], wbf[...], preferred_element_type=F32)
    if act == "gelu":
        acc = jax.nn.gelu(acc)
    if resid:
        acc = r_ref[0] + g_ref[0] * acc
    o_ref[0] = acc.astype(o_ref.dtype)


def _mm(x, w, *, col0=0, ncols=None, tn, act=None, resid=None, gate=None, out_dtype=F32, name="mm"):
    bx, l, k = x.shape
    ncols = w.shape[1] - col0 if ncols is None else ncols
    tm = _pick(l, (512, 256, 128, 64, 32, 16))
    nj = ncols // tn
    j0 = col0 // tn
    in_specs = [pl.BlockSpec((1, tm, k), lambda j, b, i: (b, i, 0)),
                pl.BlockSpec((k, tn), lambda j, b, i: (0, j0 + j))]
    args = [x, w]
    if resid is not None:
        in_specs.append(pl.BlockSpec((1, tm, tn), lambda j, b, i: (b, i, j)))
        if gate.shape[1] == 1:
            in_specs.append(pl.BlockSpec((1, 1, tn), lambda j, b, i: (b, 0, j)))
        else:
            in_specs.append(pl.BlockSpec((1, tm, tn), lambda j, b, i: (b, i, j)))
        args += [resid, gate]
    return pl.pallas_call(
        functools.partial(_mm_kernel, act=act, resid=resid is not None),
        grid=(nj, bx, l // tm),
        in_specs=in_specs,
        out_specs=pl.BlockSpec((1, tm, tn), lambda j, b, i: (b, i, j)),
        out_shape=jax.ShapeDtypeStruct((bx, l, ncols), out_dtype),
        scratch_shapes=[pltpu.VMEM((k, tn), BF16)],
        compiler_params=_cparams(3),
        name=name,
    )(*args)


def _lru_coeffs(xc, wrg, wig, brg, big, lam):
    heads = xc.shape[1] // LRU_BLOCK
    rs, igs = [], []
    for h in range(heads):
        xh = xc[:, h * LRU_BLOCK:(h + 1) * LRU_BLOCK].astype(BF16)
        rs.append(jnp.dot(xh, wrg[h], preferred_element_type=F32))
        igs.append(jnp.dot(xh, wig[h], preferred_element_type=F32))
    r = jax.nn.sigmoid(jnp.concatenate(rs, axis=1) + brg)
    ig = jax.nn.sigmoid(jnp.concatenate(igs, axis=1) + big)
    log_a = -LRU_C * r * jax.nn.softplus(-lam)
    a = jnp.exp(log_a)
    b = jnp.sqrt(1.0 - jnp.exp(2.0 * log_a)) * (ig * xc)
    return a, b


def _lru_seq_kernel(rec_ref, gate_ref, h0_ref, cv0_ref, cw_ref, cb_ref, wrg_ref, wig_ref, brg_ref,
                    big_ref, lam_ref, y_ref, hout_ref, ebuf, hcar, wrgb, wigb, *, tc):
    c = pl.program_id(2)

    @pl.when(c == 0)
    def _():
        ebuf[5:8, :] = cv0_ref[0]
        hcar[...] = h0_ref[0]
        wrgb[...] = wrg_ref[...].astype(BF16)
        wigb[...] = wig_ref[...].astype(BF16)

    rec = rec_ref[0]
    ebuf[8:8 + tc, :] = rec
    cw = cw_ref[...]
    xc = cb_ref[...] + ebuf[5:5 + tc, :] * cw[0:1] + ebuf[6:6 + tc, :] * cw[1:2] \
        + ebuf[7:7 + tc, :] * cw[2:3] + rec * cw[3:4]
    a, b = _lru_coeffs(xc, wrgb, wigb, brg_ref[...], big_ref[...], lam_ref[...])

    row = lax.broadcasted_iota(I32, a.shape, 0)
    s = 1
    while s < tc:
        m = row >= s
        a_sh = pltpu.roll(a, s, 0)
        b_sh = pltpu.roll(b, s, 0)
        b = jnp.where(m, a * b_sh + b, b)
        a = jnp.where(m, a * a_sh, a)
        s *= 2
    h = a * hcar[...] + b
    y_ref[0] = (gate_ref[0] * h).astype(y_ref.dtype)
    hlast = h[tc - 1:tc, :]
    hcar[...] = hlast
    hout_ref[0] = hlast
    ebuf[5:8, :] = rec[tc - 3:tc, :]


def _lru_seq(rec, gate, h0, cv0, conv_w, conv_b, w_rg, w_ig, b_rg, b_ig, lam):
    bx, l, ch = rec.shape
    tc = _pick(l, (256, 128, 64, 32, 16, 8))
    heads = ch // LRU_BLOCK
    hpt = 7 if heads % 7 == 0 else (5 if heads % 5 == 0 else 1)
    ct = hpt * LRU_BLOCK
    nct = ch // ct
    row = lambda a: a.reshape(1, ch)
    cspec = lambda r: pl.BlockSpec((r, ct), lambda b, j, c: (0, j))
    return pl.pallas_call(
        functools.partial(_lru_seq_kernel, tc=tc),
        grid=(bx, nct, l // tc),
        in_specs=[pl.BlockSpec((1, tc, ct), lambda b, j, c: (b, c, j)),
                  pl.BlockSpec((1, tc, ct), lambda b, j, c: (b, c, j)),
                  pl.BlockSpec((1, 1, ct), lambda b, j, c: (b, 0, j)),
                  pl.BlockSpec((1, 3, ct), lambda b, j, c: (b, 0, j)),
                  cspec(4), cspec(1),
                  pl.BlockSpec((hpt, LRU_BLOCK, LRU_BLOCK), lambda b, j, c: (j, 0, 0)),
                  pl.BlockSpec((hpt, LRU_BLOCK, LRU_BLOCK), lambda b, j, c: (j, 0, 0)),
                  cspec(1), cspec(1), cspec(1)],
        out_specs=[pl.BlockSpec((1, tc, ct), lambda b, j, c: (b, c, j)),
                   pl.BlockSpec((1, 1, ct), lambda b, j, c: (b, 0, j))],
        out_shape=[jax.ShapeDtypeStruct((bx, l, ch), BF16),
                   jax.ShapeDtypeStruct((bx, 1, ch), F32)],
        scratch_shapes=[pltpu.VMEM((8 + tc, ct), F32), pltpu.VMEM((1, ct), F32),
                        pltpu.VMEM((hpt, LRU_BLOCK, LRU_BLOCK), BF16),
                        pltpu.VMEM((hpt, LRU_BLOCK, LRU_BLOCK), BF16)],
        compiler_params=_cparams(3),
        name="lru_seq",
    )(rec, gate, h0.reshape(bx, 1, ch), cv0, conv_w, row(conv_b), w_rg, w_ig, row(b_rg), row(b_ig), row(lam))


def _lru_step_kernel(rec_ref, gate_ref, h0_ref, cv0_ref, cw_ref, cb_ref, wrg_ref, wig_ref, brg_ref,
                     big_ref, lam_ref, y_ref, hout_ref):
    rec = rec_ref[...]
    cw = cw_ref[...]
    xc = cb_ref[...] + cv0_ref[0] * cw[0:1] + cv0_ref[1] * cw[1:2] + cv0_ref[2] * cw[2:3] + rec * cw[3:4]
    a, b = _lru_coeffs(xc, wrg_ref[...].astype(BF16), wig_ref[...].astype(BF16),
                       brg_ref[...], big_ref[...], lam_ref[...])
    h = a * h0_ref[...] + b
    y_ref[...] = (gate_ref[...] * h).astype(y_ref.dtype)
    hout_ref[...] = h


def _lru_step(rec, gate, h0, cv0_t, conv_w, conv_b, w_rg, w_ig, b_rg, b_ig, lam):
    bx, ch = rec.shape
    heads = ch // LRU_BLOCK
    hpt = 7 if heads % 7 == 0 else (5 if heads % 5 == 0 else 1)
    ct = hpt * LRU_BLOCK
    row = lambda a: a.reshape(1, ch)
    bspec = pl.BlockSpec((bx, ct), lambda j: (0, j))
    cspec = lambda r: pl.BlockSpec((r, ct), lambda j: (0, j))
    wspec = pl.BlockSpec((hpt, LRU_BLOCK, LRU_BLOCK), lambda j: (j, 0, 0))
    return pl.pallas_call(
        _lru_step_kernel,
        grid=(ch // ct,),
        in_specs=[bspec, bspec, bspec, pl.BlockSpec((3, bx, ct), lambda j: (0, 0, j)),
                  cspec(4), cspec(1), wspec, wspec, cspec(1), cspec(1), cspec(1)],
        out_specs=[bspec, bspec],
        out_shape=[jax.ShapeDtypeStruct((bx, ch), BF16), jax.ShapeDtypeStruct((bx, ch), F32)],
        compiler_params=_cparams(1),
        name="lru_step",
    )(rec, gate, h0, cv0_t, conv_w, row(conv_b), w_rg, w_ig, row(b_rg), row(b_ig), row(lam))


def _pool_seq_kernel(x_ref, g_ref, sh_ref, sc_ref, gt_ref, buf0_ref, pw_ref, ps_ref, o_ref, st_ref,
                     ebuf, pwb, *, tc, pos0):
    c = pl.program_id(1)
    d = x_ref.shape[2]
    gd = d // len(POOL_WINDOWS)

    @pl.when(c == 0)
    def _():
        ebuf[1:16, :] = buf0_ref[0]
        pwb[...] = pw_ref[...].astype(BF16)

    x = x_ref[0]
    u = _modnorm(x, g_ref[...], sh_ref[0], sc_ref[0])
    ebuf[16:16 + tc, :] = u
    pos = pos0 + c * tc + lax.broadcasted_iota(I32, (tc, gd), 0)
    ys = []
    for gi, w in enumerate(POOL_WINDOWS):
        sl = slice(gi * gd, (gi + 1) * gd)
        ssum = u[:, sl]
        for j in range(1, w):
            ssum = ssum + ebuf[16 - j:16 - j + tc, sl]
        cnt = jnp.minimum(w, pos + 1).astype(F32)
        p = ssum / cnt - u[:, sl]
        ys.append(jnp.dot(p.astype(BF16), pwb[gi], preferred_element_type=F32))
    y = jnp.concatenate(ys, axis=1) * ps_ref[...]
    o_ref[0] = x + gt_ref[0] * y
    st_ref[0] = ebuf[tc + 1:tc + 16, :]
    ebuf[0:16, :] = ebuf[tc:tc + 16, :]


def _pool_seq(x, g, shift, scale, gate, buf0, pool_w, pool_scale, pos0):
    bx, l, d = x.shape
    tc = _pick(l, (256, 128, 64, 32, 16))
    ng, gd, _ = pool_w.shape
    mspec = lambda m: (pl.BlockSpec((1, 1, d), lambda b, c: (b, 0, 0)) if m.shape[1] == 1
                       else pl.BlockSpec((1, tc, d), lambda b, c: (b, c, 0)))
    return pl.pallas_call(
        functools.partial(_pool_seq_kernel, tc=tc, pos0=pos0),
        grid=(bx, l // tc),
        in_specs=[pl.BlockSpec((1, tc, d), lambda b, c: (b, c, 0)),
                  pl.BlockSpec((1, d), lambda b, c: (0, 0)),
                  mspec(shift), mspec(scale), mspec(gate),
                  pl.BlockSpec((1, POOL_BUF, d), lambda b, c: (b, 0, 0)),
                  pl.BlockSpec((ng, gd, gd), lambda b, c: (0, 0, 0)),
                  pl.BlockSpec((1, d), lambda b, c: (0, 0))],
        out_specs=[pl.BlockSpec((1, tc, d), lambda b, c: (b, c, 0)),
                   pl.BlockSpec((1, POOL_BUF, d), lambda b, c: (b, 0, 0))],
        out_shape=[jax.ShapeDtypeStruct((bx, l, d), F32),
                   jax.ShapeDtypeStruct((bx, POOL_BUF, d), F32)],
        scratch_shapes=[pltpu.VMEM((16 + tc, d), F32), pltpu.VMEM((ng, gd, gd), BF16)],
        compiler_params=_cparams(2),
        name="pool_seq",
    )(x, g.reshape(1, d), shift, scale, gate, buf0, pool_w, pool_scale.reshape(1, d))


def _pool_step_kernel(x_ref, g_ref, sh_ref, sc_ref, gt_ref, buf_ref, pw_ref, ps_ref, o_ref, u_ref, *, pos0):
    d = x_ref.shape[1]
    gd = d // len(POOL_WINDOWS)
    x = x_ref[...]
    u = _modnorm(x, g_ref[...], sh_ref[...], sc_ref[...])
    ys = []
    for gi, w in enumerate(POOL_WINDOWS):
        sl = slice(gi * gd, (gi + 1) * gd)
        ssum = u[:, sl]
        for j in range(1, w):
            ssum = ssum + buf_ref[POOL_BUF - j, :, sl]
        cnt = float(min(w, pos0 + 1))
        p = ssum / cnt - u[:, sl]
        ys.append(_bdot(p, pw_ref[gi]))
    y = jnp.concatenate(ys, axis=1) * ps_ref[...]
    o_ref[...] = x + gt_ref[...] * y
    u_ref[...] = u


def _pool_step(x, g, shift, scale, gate, buf_t, pool_w, pool_scale, pos0):
    bx, d = x.shape
    ng, gd, _ = pool_w.shape
    full = pl.BlockSpec((bx, d), lambda i: (0, 0))
    vec = pl.BlockSpec((1, d), lambda i: (0, 0))
    return pl.pallas_call(
        functools.partial(_pool_step_kernel, pos0=pos0),
        grid=(1,),
        in_specs=[full, vec, full, full, full,
                  pl.BlockSpec((POOL_BUF, bx, d), lambda i: (0, 0, 0)),
                  pl.BlockSpec((ng, gd, gd), lambda i: (0, 0, 0)), vec],
        out_specs=[full, full],
        out_shape=[jax.ShapeDtypeStruct((bx, d), F32), jax.ShapeDtypeStruct((bx, d), F32)],
        compiler_params=_cparams(1),
        name="pool_step",
    )(x, g.reshape(1, d), shift, scale, gate, buf_t, pool_w, pool_scale.reshape(1, d))


def _first_max(v, idx, n):
    m = jnp.max(v, axis=0, keepdims=True)
    first = jnp.min(jnp.where(v == m, idx, n), axis=0, keepdims=True)
    return jnp.where(idx == first, 1.0, 0.0), m, first


def _stack_rows(rows, dtype):
    n = rows[0].shape[1]
    r8 = lax.broadcasted_iota(I32, (8, n), 0)
    out = jnp.zeros((8, n), dtype)
    for k, r in enumerate(rows):
        out = jnp.where(r8 == k, jnp.broadcast_to(r.astype(dtype), (8, n)), out)
    return out


def _router_kernel(lg_ref, bias_ref, eid_ref, rank_ref, w_ref, cnt_ref, carry, *, tr):
    i = pl.program_id(0)

    @pl.when(i == 0)
    def _():
        carry[...] = jnp.zeros_like(carry)

    ne = lg_ref.shape[0]
    gs = ne // N_GROUPS
    s = jax.nn.sigmoid(lg_ref[...])
    biased = s + bias_ref[...]
    neg = -jnp.inf
    idx8 = lax.broadcasted_iota(I32, (gs, tr), 0)
    gidx = lax.broadcasted_iota(I32, (N_GROUPS, tr), 0)
    gsc = jnp.zeros((N_GROUPS, tr), F32)
    for g in range(N_GROUPS):
        blk = biased[g * gs:(g + 1) * gs, :]
        oh, m1, _ = _first_max(blk, idx8, gs)
        m2 = jnp.max(jnp.where(oh > 0, neg, blk), axis=0, keepdims=True)
        gsc = jnp.where(gidx == g, jnp.broadcast_to(m1 + m2, (N_GROUPS, tr)), gsc)
    gsel = jnp.zeros((N_GROUPS, tr), F32)
    for _ in range(TOPK_GROUPS):
        oh, _, _ = _first_max(gsc, gidx, N_GROUPS)
        gsel = gsel + oh
        gsc = jnp.where(oh > 0, neg, gsc)
    emask = jnp.concatenate([jnp.broadcast_to(gsel[g:g + 1, :], (gs, tr)) for g in range(N_GROUPS)], axis=0)
    masked = jnp.where(emask > 0, biased, neg)
    eidx = lax.broadcasted_iota(I32, (ne, tr), 0)
    ohs, ws, ids = [], [], []
    for _ in range(TOP_K):
        oh, _, first = _first_max(masked, eidx, ne)
        ohs.append(oh)
        ids.append(first)
        ws.append(jnp.sum(oh * s, axis=0, keepdims=True))
        masked = jnp.where(oh > 0, neg, masked)
    wsum = ws[0]
    cm = ohs[0]
    for k in range(1, TOP_K):
        wsum = wsum + ws[k]
        cm = cm + ohs[k]
    tri = (lax.broadcasted_iota(I32, (tr, tr), 0) < lax.broadcasted_iota(I32, (tr, tr), 1))
    prefix = jnp.dot(cm.astype(BF16), jnp.where(tri, 1.0, 0.0).astype(BF16),
                     preferred_element_type=F32) + carry[...]
    rk = [jnp.sum(ohs[k] * prefix, axis=0, keepdims=True) for k in range(TOP_K)]
    rank_ref[...] = _stack_rows(rk, F32).astype(I32)
    eid_ref[...] = _stack_rows(ids, I32)
    w_ref[...] = _stack_rows([w / wsum * ROUTED_SCALE for w in ws], F32)
    carry[...] = carry[...] + jnp.sum(cm, axis=1, keepdims=True)
    cnt_ref[...] = jnp.broadcast_to(carry[...], cnt_ref.shape)


def _router(logits_t, bias):
    ne, t = logits_t.shape
    tr = _pick(t, (640, 512, 256, 128))
    o8 = lambda dt: jax.ShapeDtypeStruct((8, t), dt)
    spec8 = pl.BlockSpec((8, tr), lambda i: (0, i))
    return pl.pallas_call(
        functools.partial(_router_kernel, tr=tr),
        grid=(t // tr,),
        in_specs=[pl.BlockSpec((ne, tr), lambda i: (0, i)), pl.BlockSpec((ne, 1), lambda i: (0, 0))],
        out_specs=[spec8, spec8, spec8, pl.BlockSpec((ne, LANE), lambda i: (0, 0))],
        out_shape=[o8(I32), o8(I32), o8(F32), jax.ShapeDtypeStruct((ne, LANE), F32)],
        scratch_shapes=[pltpu.VMEM((ne, 1), F32)],
        compiler_params=_cparams(1),
        name="router",
    )(logits_t, bias.reshape(ne, 1))


def _dispatch_kernel(zb_ref, dest_ref, x_ref, xs_ref, zbuf, sem, zsem, *, td, tm, ne):
    i = pl.program_id(0)

    @pl.when(i == 0)
    def _():
        zbuf[...] = jnp.zeros_like(zbuf)

        def zero_copy(e):
            return pltpu.make_async_copy(zbuf, xs_ref.at[pl.ds(zb_ref[e] * tm, tm), :], zsem)

        def zstart(e, c):
            @pl.when(zb_ref[e] >= 0)
            def _():
                zero_copy(e).start()
            return c

        def zwait(e, c):
            @pl.when(zb_ref[e] >= 0)
            def _():
                zero_copy(e).wait()
            return c

        lax.fori_loop(0, ne, zstart, 0)
        lax.fori_loop(0, ne, zwait, 0)

    def row_copy(t, k):
        return pltpu.make_async_copy(x_ref.at[pl.ds(t, 1), :],
                                     xs_ref.at[pl.ds(dest_ref[0, k, t], 1), :], sem)

    def start(t, c):
        for k in range(TOP_K):
            row_copy(t, k).start()
        return c

    def wait(t, c):
        for k in range(TOP_K):
            row_copy(t, k).wait()
        return c

    lax.fori_loop(0, td, start, 0)
    lax.fori_loop(0, td, wait, 0)


def _dispatch(zero_blk, dest_tiles, packed, r_max, tm):
    t_all, width = packed.shape
    nt, _, td = dest_tiles.shape
    ne = zero_blk.shape[0]
    return pl.pallas_call(
        functools.partial(_dispatch_kernel, td=td, tm=tm, ne=ne),
        grid_spec=pltpu.PrefetchScalarGridSpec(
            num_scalar_prefetch=1,
            grid=(nt,),
            in_specs=[pl.BlockSpec((1, 8, td), lambda i, zb: (i, 0, 0), memory_space=pltpu.SMEM),
                      pl.BlockSpec((td, width), lambda i, zb: (i, 0))],
            out_specs=pl.BlockSpec(memory_space=pl.ANY),
            scratch_shapes=[pltpu.VMEM((tm, width), U32), pltpu.SemaphoreType.DMA, pltpu.SemaphoreType.DMA]),
        out_shape=jax.ShapeDtypeStruct((r_max, width), U32),
        compiler_params=_cparams(1),
        name="dispatch",
    )(zero_blk, dest_tiles, packed)


def _expert_kernel(be_ref, nr_ref, xb_ref, x_ref, wg_ref, wu_ref, wd_ref, wgt_ref, wut_ref, wdt_ref,
                   o_ref, xbf, wgb, wub, wdb, *, tm, sub, nf_main, has_tail):
    del be_ref, xb_ref
    b = pl.program_id(0)
    f = pl.program_id(1)
    n = nr_ref[b]
    nsub = tm // sub

    def mlp(xs, wg, wu, wd):
        g = jnp.dot(xs, wg, preferred_element_type=F32)
        u = jnp.dot(xs, wu, preferred_element_type=F32)
        h = (_silu(g) * u).astype(BF16)
        return jnp.dot(h, wd, preferred_element_type=F32)

    for s in range(nsub):
        rows = slice(s * sub, (s + 1) * sub)

        @pl.when((f == 0) & (s * sub < n))
        def _():
            lo, hi = _unpack_pair(x_ref[rows, :])
            xbf[rows, :] = jnp.concatenate([lo, hi], axis=1).astype(BF16)

    @pl.when((n > 0) & (f < nf_main))
    def _():
        wgb[...] = wg_ref[0].astype(BF16)
        wub[...] = wu_ref[0].astype(BF16)
        wdb[...] = wd_ref[0].astype(BF16)

    for s in range(nsub):
        rows = slice(s * sub, (s + 1) * sub)

        @pl.when((n > 0) & (f == 0) & (s * sub < n))
        def _():
            o_ref[rows, :] = mlp(xbf[rows, :], wgb[...], wub[...], wdb[...])

        @pl.when((n > 0) & (f == 0) & (s * sub >= n))
        def _():
            o_ref[rows, :] = jnp.zeros((sub, o_ref.shape[1]), F32)

        @pl.when((f > 0) & (f < nf_main) & (s * sub < n))
        def _():
            o_ref[rows, :] += mlp(xbf[rows, :], wgb[...], wub[...], wdb[...])

        if has_tail:
            @pl.when((f == nf_main) & (s * sub < n))
            def _():
                o_ref[rows, :] += mlp(xbf[rows, :], wgt_ref[0].astype(BF16), wut_ref[0].astype(BF16),
                                      wdt_ref[0].astype(BF16))


def _experts(block_e, nrows, xblk, xs, w_gate, w_up, w_down, *, tm, sub, n_blocks, out_rows):
    ne, d, f = w_gate.shape
    width = xs.shape[1]
    tf = 256 if f >= 256 else f
    nf_main = f // tf
    tail = f - nf_main * tf
    has_tail = tail > 0
    nf = nf_main + (1 if has_tail else 0)
    tt = tail if has_tail else LANE
    tail_blk = (f - tt) // tt
    fm = lambda fi: jnp.minimum(fi, nf_main - 1)
    in_specs = [pl.BlockSpec((tm, width), lambda b, fi, be, nr, xb: (xb[b], 0)),
                pl.BlockSpec((1, d, tf), lambda b, fi, be, nr, xb: (be[b], 0, fm(fi))),
                pl.BlockSpec((1, d, tf), lambda b, fi, be, nr, xb: (be[b], 0, fm(fi))),
                pl.BlockSpec((1, tf, d), lambda b, fi, be, nr, xb: (be[b], fm(fi), 0)),
                pl.BlockSpec((1, d, tt), lambda b, fi, be, nr, xb: (be[b], 0, tail_blk)),
                pl.BlockSpec((1, d, tt), lambda b, fi, be, nr, xb: (be[b], 0, tail_blk)),
                pl.BlockSpec((1, tt, d), lambda b, fi, be, nr, xb: (be[b], tail_blk, 0))]
    return pl.pallas_call(
        functools.partial(_expert_kernel, tm=tm, sub=sub, nf_main=nf_main, has_tail=has_tail),
        grid_spec=pltpu.PrefetchScalarGridSpec(
            num_scalar_prefetch=3,
            grid=(n_blocks, nf),
            in_specs=in_specs,
            out_specs=pl.BlockSpec((tm, d), lambda b, fi, be, nr, xb: (xb[b], 0)),
            scratch_shapes=[pltpu.VMEM((tm, d), BF16), pltpu.VMEM((d, tf), BF16),
                            pltpu.VMEM((d, tf), BF16), pltpu.VMEM((tf, d), BF16)]),
        out_shape=jax.ShapeDtypeStruct((out_rows, d), F32),
        compiler_params=_cparams(2),
        name="experts",
    )(block_e, nrows, xblk, xs, w_gate, w_up, w_down, w_gate, w_up, w_down)


def _combine_kernel(dest_ref, x_ref, gt_ref, w_ref, ysh_ref, yb_ref, o_ref, gbuf, sem, *, tmc):
    def row_copy(t, k):
        return pltpu.make_async_copy(yb_ref.at[pl.ds(dest_ref[0, k, t], 1), :],
                                     gbuf.at[k, pl.ds(t, 1), :], sem)

    def start(t, c):
        for k in range(TOP_K):
            row_copy(t, k).start()
        return c

    def wait(t, c):
        for k in range(TOP_K):
            row_copy(t, k).wait()
        return c

    lax.fori_loop(0, tmc, start, 0)
    lax.fori_loop(0, tmc, wait, 0)
    w = w_ref[...]
    acc = w[:, 0:1] * gbuf[0]
    for k in range(1, TOP_K):
        acc = acc + w[:, k:k + 1] * gbuf[k]
    o_ref[0] = x_ref[0] + gt_ref[0] * (acc + ysh_ref[...])


def _combine(x, gate, dest_tiles, w_tok, ysh, yb, row0):
    bx, l, d = x.shape
    tmc = dest_tiles.shape[2]
    nl = l // tmc
    blk0 = row0 // tmc
    gspec = (pl.BlockSpec((1, 1, d), lambda b, i: (b, 0, 0)) if gate.shape[1] == 1
             else pl.BlockSpec((1, tmc, d), lambda b, i: (b, i, 0)))
    return pl.pallas_call(
        functools.partial(_combine_kernel, tmc=tmc),
        grid=(bx, nl),
        in_specs=[pl.BlockSpec((1, 8, tmc), lambda b, i: (b * nl + i, 0, 0), memory_space=pltpu.SMEM),
                  pl.BlockSpec((1, tmc, d), lambda b, i: (b, i, 0)),
                  gspec,
                  pl.BlockSpec((tmc, 8), lambda b, i: (b * nl + i, 0)),
                  pl.BlockSpec((tmc, d), lambda b, i: (blk0 + b * nl + i, 0)),
                  pl.BlockSpec(memory_space=pl.ANY)],
        out_specs=pl.BlockSpec((1, tmc, d), lambda b, i: (b, i, 0)),
        out_shape=jax.ShapeDtypeStruct((bx, l, d), F32),
        scratch_shapes=[pltpu.VMEM((TOP_K, tmc, d), F32), pltpu.SemaphoreType.DMA],
        compiler_params=_cparams(2),
        name="combine",
    )(dest_tiles, x, gate, w_tok, ysh, yb)


def _moe_config(t_all, ne):
    avg = max(1, t_all * TOP_K // ne)
    if avg >= 512:
        return 896, 128
    tm = max(32, 2 << avg.bit_length())
    return tm, max(16, tm // 4)


def _moe(xs_trunks, mods, norm_g, router_w, router_bias, w_gate, w_up, w_down, ws_gate, ws_up, ws_down):
    ne, d, _ = w_gate.shape
    sizes = [x.shape[0] * x.shape[1] for x in xs_trunks]
    t_all = sum(sizes)
    tm, sub = _moe_config(t_all, ne)
    td =_pick(t_all, (640, 512, 256, 128, 64, 32, 16))
    tk = t_all * TOP_K
    n_blocks = (tk + ne * (tm - 1)) // tm
    r_max = n_blocks * tm

    router_wt = router_w.T
    packed, logits, row0 = None, [], 0
    for x, (sh, sc, _) in zip(xs_trunks, mods):
        packed, lg = _norm_ffn(x, norm_g, sh, sc, router_wt, row0, t_all, packed)
        logits.append(lg)
        row0 += x.shape[0] * x.shape[1]
    eid, rank, wsel, cnt = _router(jnp.concatenate(logits, axis=1), router_bias)

    counts = cnt[:, 0].astype(I32)
    nblk = (counts + tm - 1) // tm
    bend = jnp.cumsum(nblk)
    bstart = bend - nblk
    n_used = bend[-1]
    bidx = jnp.arange(n_blocks, dtype=I32)
    last = jnp.maximum(n_used - 1, 0)
    block_e = jnp.minimum(jnp.searchsorted(bend, jnp.minimum(bidx, last), side="right"), ne - 1).astype(I32)
    nrows = jnp.where(bidx < n_used, jnp.clip(counts[block_e] - (bidx - bstart[block_e]) * tm, 0, tm), 0).astype(I32)
    xblk = jnp.minimum(bidx, last).astype(I32)
    zero_blk = jnp.where(counts > 0, bend - 1, -1).astype(I32)
    dest = (bstart * tm)[eid] + rank

    def tiles(a, t):
        return a.reshape(8, t_all // t, t).transpose(1, 0, 2)

    xs_sorted = _dispatch(zero_blk, tiles(dest, td), packed, r_max, tm)
    yb = _experts(block_e, nrows, xblk, xs_sorted, w_gate, w_up, w_down,
                  tm=tm, sub=sub, n_blocks=n_blocks, out_rows=r_max)

    ns_blocks = -(-t_all // tm)
    sidx = jnp.arange(ns_blocks, dtype=I32)
    ysh = _experts(jnp.zeros((ns_blocks,), I32), jnp.clip(t_all - sidx * tm, 0, tm).astype(I32), sidx, packed,
                   ws_gate[None], ws_up[None], ws_down[None], tm=tm, sub=sub, n_blocks=ns_blocks, out_rows=t_all)

    outs, row0 = [], 0
    for x, (_, _, gt) in zip(xs_trunks, mods):
        nt = x.shape[0] * x.shape[1]
        tmc = _pick(x.shape[1], (256, 128, 64, 32, 16))
        dest_c = dest[:, row0:row0 + nt].reshape(8, nt // tmc, tmc).transpose(1, 0, 2)
        outs.append(_combine(x, gt, dest_c, wsel[:, row0:row0 + nt].T, ysh, yb, row0))
        row0 += nt
    return outs


def kernel(x_prompt, x_sample, state_lru_h, state_lru_conv, state_pool, c_prompt, c_sample, ada_w, ada_b, norm_mix, norm_ffn, norm_out, lru_w_in, lru_conv_w, lru_conv_b, lru_w_rg, lru_b_rg, lru_w_ig, lru_b_ig, lru_lambda, lru_w_out, pool_w, pool_scale, router_w, router_bias, exp_w_gate, exp_w_up, exp_w_down, shared_w_gate, shared_w_up, shared_w_down):
    bp, seq, d = x_prompt.shape
    bs = x_sample.shape[0]
    depth = ada_w.shape[0]
    ch = lru_w_in.shape[2] // 2
    assert x_sample.shape[1] == 1 and seq >= 16

    bc = -(-(bp + bs) // 8) * 8
    c_all = jnp.concatenate([c_prompt, c_sample, jnp.zeros((bc - bp - bs, d), F32)], axis=0)
    mod = _ada(c_all, ada_w, ada_b)

    def mods(i):
        mp = [mod[i, :bp, k * d:(k + 1) * d].reshape(bp, 1, d) for k in range(6)]
        ms = [mod[i, bp:bp + bs, k * d:(k + 1) * d].reshape(1, bs, d) for k in range(6)]
        return mp, ms

    xp = x_prompt
    xs = x_sample.reshape(1, bs, d)
    tn_in = _pick(ch, (896, 640, 512, 384, 256, 128))
    tn_out = _pick(d, (512, 256, 128))
    new_h_p, new_cv_p, new_pb_p, new_h_s, new_cv_s, new_pb_s = [], [], [], [], [], []
    for i in range(depth):
        mp, ms = mods(i)
        j = i // 2
        if i % 2 == 0:
            w_in = lru_w_in[j]
            lru = (lru_conv_w[j], lru_conv_b[j], lru_w_rg[j], lru_w_ig[j], lru_b_rg[j], lru_b_ig[j], lru_lambda[j])
            u = _norm_mix(xp, norm_mix[i], mp[0], mp[1])
            gate = _mm(u, w_in, col0=0, ncols=ch, tn=tn_in, act="gelu", name="lru_in_gate")
            rec = _mm(u, w_in, col0=ch, ncols=ch, tn=tn_in, name="lru_in_rec")
            ypre, h_last = _lru_seq(rec, gate, jnp.zeros((bp, ch), F32), jnp.zeros((bp, 3, ch), F32), *lru)
            xp = _mm(ypre, lru_w_out[j], tn=tn_out, resid=xp, gate=mp[2], name="lru_out")
            new_h_p.append(h_last.reshape(bp, ch))
            new_cv_p.append(rec[:, seq - 3:, :])
            u = _norm_mix(xs, norm_mix[i], ms[0], ms[1])
            gate = _mm(u, w_in, col0=0, ncols=ch, tn=tn_in, act="gelu", name="lru_in_gate")
            rec = _mm(u, w_in, col0=ch, ncols=ch, tn=tn_in, name="lru_in_rec")
            cv0 = state_lru_conv[j]
            ypre, h_new = _lru_step(rec[0], gate[0], state_lru_h[j], cv0.transpose(1, 0, 2), *lru)
            xs = _mm(ypre[None], lru_w_out[j], tn=tn_out, resid=xs, gate=ms[2], name="lru_out")
            new_h_s.append(h_new)
            new_cv_s.append(jnp.concatenate([cv0[:, 1:, :], rec[0][:, None, :]], axis=1))
        else:
            xp, pb = _pool_seq(xp, norm_mix[i], mp[0], mp[1], mp[2], jnp.zeros((bp, POOL_BUF, d), F32),
                               pool_w[j], pool_scale[j], 0)
            new_pb_p.append(pb)
            buf0 = state_pool[j]
            xs2, u_s = _pool_step(xs[0], norm_mix[i], ms[0][0], ms[1][0], ms[2][0], buf0.transpose(1, 0, 2),
                                  pool_w[j], pool_scale[j], PAST_LEN)
            xs = xs2[None]
            new_pb_s.append(jnp.concatenate([buf0[:, 1:, :], u_s[:, None, :]], axis=1))
        xp, xs = _moe([xp, xs], [(mp[3], mp[4], mp[5]), (ms[3], ms[4], ms[5])], norm_ffn[i],
                      router_w[i], router_bias[i], exp_w_gate[i], exp_w_up[i], exp_w_down[i],
                      shared_w_gate[i], shared_w_up[i], shared_w_down[i])
    y_prompt = _norm_out(xp, norm_out)
    y_sample = _norm_out(xs, norm_out).reshape(bs, 1, d)
    return (y_prompt, y_sample, jnp.stack(new_h_p), jnp.stack(new_cv_p), jnp.stack(new_pb_p),
            jnp.stack(new_h_s), jnp.stack(new_cv_s), jnp.stack(new_pb_s))
```

```python
import functools

import jax
import jax.numpy as jnp
from jax import lax
from jax.experimental import pallas as pl
from jax.experimental.pallas import tpu as pltpu

F32 = jnp.float32
BF16 = jnp.bfloat16
I32 = jnp.int32
U32 = jnp.uint32

EPS = 1e-6
LRU_C = 8.0
LRU_BLOCK = 128
POOL_WINDOWS = (2, 4, 8, 16)
POOL_BUF = max(POOL_WINDOWS) - 1
TOP_K = 6
N_GROUPS = 8
TOPK_GROUPS = 4
ROUTED_SCALE = 2.5
PAST_LEN = 16384

V7X_VMEM_LIMIT_BYTES = 58 * 1024 * 1024
LANE = 128


def _cparams(n_axes):
    return pltpu.CompilerParams(dimension_semantics=("arbitrary",) * n_axes,
                                vmem_limit_bytes=V7X_VMEM_LIMIT_BYTES)


def _pick(n, candidates):
    for c in candidates:
        if c <= n and n % c == 0:
            return c
    return n


def _bdot(a, b):
    return jnp.dot(a.astype(BF16), b.astype(BF16), preferred_element_type=F32)


def _silu(x):
    return x * jax.nn.sigmoid(x)


def _ada_kernel(c_ref, w_ref, b_ref, o_ref):
    s = _silu(c_ref[...])
    o_ref[0] = _bdot(s, w_ref[0]) + b_ref[0]


def _ada(c_all, ada_w, ada_b):
    depth, d, n = ada_w.shape
    bc = c_all.shape[0]
    tn = _pick(n, (1024, 512, 256, 128))
    return pl.pallas_call(
        _ada_kernel,
        grid=(depth, n // tn),
        in_specs=[pl.BlockSpec((bc, d), lambda l, j: (0, 0)),
                  pl.BlockSpec((1, d, tn), lambda l, j: (l, 0, j)),
                  pl.BlockSpec((1, 1, tn), lambda l, j: (l, 0, j))],
        out_specs=pl.BlockSpec((1, bc, tn), lambda l, j: (l, 0, j)),
        out_shape=jax.ShapeDtypeStruct((depth, bc, n), F32),
        compiler_params=_cparams(2),
        name="ada_mod",
    )(c_all, ada_w, ada_b.reshape(depth, 1, n))


def _rms(x, g):
    return x * lax.rsqrt(jnp.mean(x * x, axis=-1, keepdims=True) + EPS) * g


def _modnorm(x, g, shift, scale):
    return _rms(x, g) * (1.0 + scale) + shift


def _mod_spec(mod, tm):
    d = mod.shape[-1]
    if mod.shape[1] == 1:
        return pl.BlockSpec((1, 1, d), lambda b, i: (b, 0, 0))
    return pl.BlockSpec((1, tm, d), lambda b, i: (b, i, 0))


def _norm_mix_kernel(x_ref, g_ref, sh_ref, sc_ref, o_ref):
    o_ref[0] = _modnorm(x_ref[0], g_ref[...], sh_ref[0], sc_ref[0]).astype(o_ref.dtype)


def _norm_mix(x, g, shift, scale):
    bx, l, d = x.shape
    tm = _pick(l, (512, 256, 128, 64, 32, 16))
    return pl.pallas_call(
        _norm_mix_kernel,
        grid=(bx, l // tm),
        in_specs=[pl.BlockSpec((1, tm, d), lambda b, i: (b, i, 0)),
                  pl.BlockSpec((1, d), lambda b, i: (0, 0)),
                  _mod_spec(shift, tm), _mod_spec(scale, tm)],
        out_specs=pl.BlockSpec((1, tm, d), lambda b, i: (b, i, 0)),
        out_shape=jax.ShapeDtypeStruct((bx, l, d), BF16),
        compiler_params=_cparams(2),
        name="norm_mix",
    )(x, g.reshape(1, d), shift, scale)


def _norm_out_kernel(x_ref, g_ref, o_ref):
    o_ref[0] = _rms(x_ref[0], g_ref[...])


def _norm_out(x, g):
    bx, l, d = x.shape
    tm = _pick(l, (512, 256, 128, 64, 32, 16, 8))
    return pl.pallas_call(
        _norm_out_kernel,
        grid=(bx, l // tm),
        in_specs=[pl.BlockSpec((1, tm, d), lambda b, i: (b, i, 0)),
                  pl.BlockSpec((1, d), lambda b, i: (0, 0))],
        out_specs=pl.BlockSpec((1, tm, d), lambda b, i: (b, i, 0)),
        out_shape=jax.ShapeDtypeStruct((bx, l, d), F32),
        compiler_params=_cparams(2),
        name="norm_out",
    )(x, g.reshape(1, d))


def _pack_pair(lo, hi):
    lo_b = lax.bitcast_convert_type(lo.astype(BF16).astype(F32), U32)
    hi_b = lax.bitcast_convert_type(hi.astype(BF16).astype(F32), U32)
    return lax.shift_right_logical(lo_b, jnp.uint32(16)) | (hi_b & jnp.uint32(0xFFFF0000))


def _unpack_pair(word):
    lo = lax.bitcast_convert_type(lax.shift_left(word, jnp.uint32(16)), F32)
    hi = lax.bitcast_convert_type(word & jnp.uint32(0xFFFF0000), F32)
    return lo, hi


def _norm_ffn_kernel(x_ref, g_ref, sh_ref, sc_ref, rwt_ref, up_ref, lg_ref):
    u = _modnorm(x_ref[0], g_ref[...], sh_ref[0], sc_ref[0])
    half = u.shape[1] // 2
    up_ref[...] = _pack_pair(u[:, :half], u[:, half:])
    lg_ref[...] = lax.dot_general(rwt_ref[...], u, (((1,), (1,)), ((), ())),
                                  precision=lax.Precision.HIGHEST, preferred_element_type=F32)


def _norm_ffn(x, g, shift, scale, router_wt):
    bx, l, d = x.shape
    e = router_wt.shape[0]
    tm = _pick(l, (512, 256, 128))
    nl = l // tm
    return pl.pallas_call(
        _norm_ffn_kernel,
        grid=(bx, nl),
        in_specs=[pl.BlockSpec((1, tm, d), lambda b, i: (b, i, 0)),
                  pl.BlockSpec((1, d), lambda b, i: (0, 0)),
                  _mod_spec(shift, tm), _mod_spec(scale, tm),
                  pl.BlockSpec((e, d), lambda b, i: (0, 0))],
        out_specs=[pl.BlockSpec((tm, d // 2), lambda b, i: (b * nl + i, 0)),
                   pl.BlockSpec((e, tm), lambda b, i: (0, b * nl + i))],
        out_shape=[jax.ShapeDtypeStruct((bx * l, d // 2), U32),
                   jax.ShapeDtypeStruct((e, bx * l), F32)],
        compiler_params=_cparams(2),
        name="norm_ffn",
    )(x, g.reshape(1, d), shift, scale, router_wt)


def _mm_kernel(*refs, act, resid):
    if resid:
        x_ref, w_ref, r_ref, g_ref, o_ref, wbf = refs
    else:
        x_ref, w_ref, o_ref, wbf = refs

    @pl.when((pl.program_id(1) == 0) & (pl.program_id(2) == 0))
    def _():
        wbf[...] = w_ref[...].astype(BF16)

    acc = jnp.dot(x_ref[0], wbf[...], preferred_element_type=F32)
    if act == "gelu":
        acc = jax.nn.gelu(acc)
    if resid:
        acc = r_ref[0] + g_ref[0] * acc
    o_ref[0] = acc.astype(o_ref.dtype)


def _mm(x, w, *, col0=0, ncols=None, tn, act=None, resid=None, gate=None, out_dtype=F32, name="mm"):
    bx, l, k = x.shape
    ncols = w.shape[1] - col0 if ncols is None else ncols
    tm = _pick(l, (512, 256, 128, 64, 32, 16))
    nj = ncols // tn
    j0 = col0 // tn
    in_specs = [pl.BlockSpec((1, tm, k), lambda j, b, i: (b, i, 0)),
                pl.BlockSpec((k, tn), lambda j, b, i: (0, j0 + j))]
    args = [x, w]
    if resid is not None:
        in_specs.append(pl.BlockSpec((1, tm, tn), lambda j, b, i: (b, i, j)))
        if gate.shape[1] == 1:
            in_specs.append(pl.BlockSpec((1, 1, tn), lambda j, b, i: (b, 0, j)))
        else:
            in_specs.append(pl.BlockSpec((1, tm, tn), lambda j, b, i: (b, i, j)))
        args += [resid, gate]
    return pl.pallas_call(
        functools.partial(_mm_kernel, act=act, resid=resid is not None),
        grid=(nj, bx, l // tm),
        in_specs=in_specs,
        out_specs=pl.BlockSpec((1, tm, tn), lambda j, b, i: (b, i, j)),
        out_shape=jax.ShapeDtypeStruct((bx, l, ncols), out_dtype),
        scratch_shapes=[pltpu.VMEM((k, tn), BF16)],
        compiler_params=_cparams(3),
        name=name,
    )(*args)


def _lru_coeffs(xc, wrg, wig, brg, big, lam):
    heads = xc.shape[1] // LRU_BLOCK
    rs, igs = [], []
    for h in range(heads):
        xh = xc[:, h * LRU_BLOCK:(h + 1) * LRU_BLOCK].astype(BF16)
        rs.append(jnp.dot(xh, wrg[h], preferred_element_type=F32))
        igs.append(jnp.dot(xh, wig[h], preferred_element_type=F32))
    r = jax.nn.sigmoid(jnp.concatenate(rs, axis=1) + brg)
    ig = jax.nn.sigmoid(jnp.concatenate(igs, axis=1) + big)
    log_a = -LRU_C * r * jax.nn.softplus(-lam)
    a = jnp.exp(log_a)
    b = jnp.sqrt(1.0 - jnp.exp(2.0 * log_a)) * (ig * xc)
    return a, b


def _lru_seq_kernel(rec_ref, gate_ref, h0_ref, cv0_ref, cw_ref, cb_ref, wrg_ref, wig_ref, brg_ref,
                    big_ref, lam_ref, y_ref, hout_ref, ebuf, hcar, wrgb, wigb, *, tc):
    c = pl.program_id(2)

    @pl.when(c == 0)
    def _():
        ebuf[5:8, :] = cv0_ref[0]
        hcar[...] = h0_ref[0]
        wrgb[...] = wrg_ref[...].astype(BF16)
        wigb[...] = wig_ref[...].astype(BF16)

    rec = rec_ref[0]
    ebuf[8:8 + tc, :] = rec
    cw = cw_ref[...]
    xc = cb_ref[...] + ebuf[5:5 + tc, :] * cw[0:1] + ebuf[6:6 + tc, :] * cw[1:2] \
        + ebuf[7:7 + tc, :] * cw[2:3] + rec * cw[3:4]
    a, b = _lru_coeffs(xc, wrgb, wigb, brg_ref[...], big_ref[...], lam_ref[...])

    row = lax.broadcasted_iota(I32, a.shape, 0)
    s = 1
    while s < tc:
        m = row >= s
        a_sh = pltpu.roll(a, s, 0)
        b_sh = pltpu.roll(b, s, 0)
        b = jnp.where(m, a * b_sh + b, b)
        a = jnp.where(m, a * a_sh, a)
        s *= 2
    h = a * hcar[...] + b
    y_ref[0] = (gate_ref[0] * h).astype(y_ref.dtype)
    hlast = h[tc - 1:tc, :]
    hcar[...] = hlast
    hout_ref[0] = hlast
    ebuf[5:8, :] = rec[tc - 3:tc, :]


def _lru_seq(rec, gate, h0, cv0, conv_w, conv_b, w_rg, w_ig, b_rg, b_ig, lam):
    bx, l, ch = rec.shape
    tc = _pick(l, (256, 128, 64, 32, 16, 8))
    heads = ch // LRU_BLOCK
    hpt = 7 if heads % 7 == 0 else (5 if heads % 5 == 0 else 1)
    ct = hpt * LRU_BLOCK
    nct = ch // ct
    row = lambda a: a.reshape(1, ch)
    cspec = lambda r: pl.BlockSpec((r, ct), lambda b, j, c: (0, j))
    return pl.pallas_call(
        functools.partial(_lru_seq_kernel, tc=tc),
        grid=(bx, nct, l // tc),
        in_specs=[pl.BlockSpec((1, tc, ct), lambda b, j, c: (b, c, j)),
                  pl.BlockSpec((1, tc, ct), lambda b, j, c: (b, c, j)),
                  pl.BlockSpec((1, 1, ct), lambda b, j, c: (b, 0, j)),
                  pl.BlockSpec((1, 3, ct), lambda b, j, c: (b, 0, j)),
                  cspec(4), cspec(1),
                  pl.BlockSpec((hpt, LRU_BLOCK, LRU_BLOCK), lambda b, j, c: (j, 0, 0)),
                  pl.BlockSpec((hpt, LRU_BLOCK, LRU_BLOCK), lambda b, j, c: (j, 0, 0)),
                  cspec(1), cspec(1), cspec(1)],
        out_specs=[pl.BlockSpec((1, tc, ct), lambda b, j, c: (b, c, j)),
                   pl.BlockSpec((1, 1, ct), lambda b, j, c: (b, 0, j))],
        out_shape=[jax.ShapeDtypeStruct((bx, l, ch), BF16),
                   jax.ShapeDtypeStruct((bx, 1, ch), F32)],
        scratch_shapes=[pltpu.VMEM((8 + tc, ct), F32), pltpu.VMEM((1, ct), F32),
                        pltpu.VMEM((hpt, LRU_BLOCK, LRU_BLOCK), BF16),
                        pltpu.VMEM((hpt, LRU_BLOCK, LRU_BLOCK), BF16)],
        compiler_params=_cparams(3),
        name="lru_seq",
    )(rec, gate, h0.reshape(bx, 1, ch), cv0, conv_w, row(conv_b), w_rg, w_ig, row(b_rg), row(b_ig), row(lam))


def _lru_step_kernel(rec_ref, gate_ref, h0_ref, cv0_ref, cw_ref, cb_ref, wrg_ref, wig_ref, brg_ref,
                     big_ref, lam_ref, y_ref, hout_ref):
    rec = rec_ref[...]
    cw = cw_ref[...]
    xc = cb_ref[...] + cv0_ref[0] * cw[0:1] + cv0_ref[1] * cw[1:2] + cv0_ref[2] * cw[2:3] + rec * cw[3:4]
    a, b = _lru_coeffs(xc, wrg_ref[...].astype(BF16), wig_ref[...].astype(BF16),
                       brg_ref[...], big_ref[...], lam_ref[...])
    h = a * h0_ref[...] + b
    y_ref[...] = (gate_ref[...] * h).astype(y_ref.dtype)
    hout_ref[...] = h


def _lru_step(rec, gate, h0, cv0_t, conv_w, conv_b, w_rg, w_ig, b_rg, b_ig, lam):
    bx, ch = rec.shape
    heads = ch // LRU_BLOCK
    hpt = 7 if heads % 7 == 0 else (5 if heads % 5 == 0 else 1)
    ct = hpt * LRU_BLOCK
    row = lambda a: a.reshape(1, ch)
    bspec = pl.BlockSpec((bx, ct), lambda j: (0, j))
    cspec = lambda r: pl.BlockSpec((r, ct), lambda j: (0, j))
    wspec = pl.BlockSpec((hpt, LRU_BLOCK, LRU_BLOCK), lambda j: (j, 0, 0))
    return pl.pallas_call(
        _lru_step_kernel,
        grid=(ch // ct,),
        in_specs=[bspec, bspec, bspec, pl.BlockSpec((3, bx, ct), lambda j: (0, 0, j)),
                  cspec(4), cspec(1), wspec, wspec, cspec(1), cspec(1), cspec(1)],
        out_specs=[bspec, bspec],
        out_shape=[jax.ShapeDtypeStruct((bx, ch), BF16), jax.ShapeDtypeStruct((bx, ch), F32)],
        compiler_params=_cparams(1),
        name="lru_step",
    )(rec, gate, h0, cv0_t, conv_w, row(conv_b), w_rg, w_ig, row(b_rg), row(b_ig), row(lam))


def _pool_seq_kernel(x_ref, g_ref, sh_ref, sc_ref, gt_ref, buf0_ref, pw_ref, ps_ref, o_ref, st_ref,
                     ebuf, pwb, *, tc, pos0):
    c = pl.program_id(1)
    d = x_ref.shape[2]
    gd = d // len(POOL_WINDOWS)

    @pl.when(c == 0)
    def _():
        ebuf[1:16, :] = buf0_ref[0]
        pwb[...] = pw_ref[...].astype(BF16)

    x = x_ref[0]
    u = _modnorm(x, g_ref[...], sh_ref[0], sc_ref[0])
    ebuf[16:16 + tc, :] = u
    pos = pos0 + c * tc + lax.broadcasted_iota(I32, (tc, gd), 0)
    ys = []
    for gi, w in enumerate(POOL_WINDOWS):
        sl = slice(gi * gd, (gi + 1) * gd)
        ssum = u[:, sl]
        for j in range(1, w):
            ssum = ssum + ebuf[16 - j:16 - j + tc, sl]
        cnt = jnp.minimum(w, pos + 1).astype(F32)
        p = ssum / cnt - u[:, sl]
        ys.append(jnp.dot(p.astype(BF16), pwb[gi], preferred_element_type=F32))
    y = jnp.concatenate(ys, axis=1) * ps_ref[...]
    o_ref[0] = x + gt_ref[0] * y
    st_ref[0] = ebuf[tc + 1:tc + 16, :]
    ebuf[0:16, :] = ebuf[tc:tc + 16, :]


def _pool_seq(x, g, shift, scale, gate, buf0, pool_w, pool_scale, pos0):
    bx, l, d = x.shape
    tc = _pick(l, (256, 128, 64, 32, 16))
    ng, gd, _ = pool_w.shape
    mspec = lambda m: (pl.BlockSpec((1, 1, d), lambda b, c: (b, 0, 0)) if m.shape[1] == 1
                       else pl.BlockSpec((1, tc, d), lambda b, c: (b, c, 0)))
    return pl.pallas_call(
        functools.partial(_pool_seq_kernel, tc=tc, pos0=pos0),
        grid=(bx, l // tc),
        in_specs=[pl.BlockSpec((1, tc, d), lambda b, c: (b, c, 0)),
                  pl.BlockSpec((1, d), lambda b, c: (0, 0)),
                  mspec(shift), mspec(scale), mspec(gate),
                  pl.BlockSpec((1, POOL_BUF, d), lambda b, c: (b, 0, 0)),
                  pl.BlockSpec((ng, gd, gd), lambda b, c: (0, 0, 0)),
                  pl.BlockSpec((1, d), lambda b, c: (0, 0))],
        out_specs=[pl.BlockSpec((1, tc, d), lambda b, c: (b, c, 0)),
                   pl.BlockSpec((1, POOL_BUF, d), lambda b, c: (b, 0, 0))],
        out_shape=[jax.ShapeDtypeStruct((bx, l, d), F32),
                   jax.ShapeDtypeStruct((bx, POOL_BUF, d), F32)],
        scratch_shapes=[pltpu.VMEM((16 + tc, d), F32), pltpu.VMEM((ng, gd, gd), BF16)],
        compiler_params=_cparams(2),
        name="pool_seq",
    )(x, g.reshape(1, d), shift, scale, gate, buf0, pool_w, pool_scale.reshape(1, d))


def _pool_step_kernel(x_ref, g_ref, sh_ref, sc_ref, gt_ref, buf_ref, pw_ref, ps_ref, o_ref, u_ref, *, pos0):
    d = x_ref.shape[1]
    gd = d // len(POOL_WINDOWS)
    x = x_ref[...]
    u = _modnorm(x, g_ref[...], sh_ref[...], sc_ref[...])
    ys = []
    for gi, w in enumerate(POOL_WINDOWS):
        sl = slice(gi * gd, (gi + 1) * gd)
        ssum = u[:, sl]
        for j in range(1, w):
            ssum = ssum + buf_ref[POOL_BUF - j, :, sl]
        cnt = float(min(w, pos0 + 1))
        p = ssum / cnt - u[:, sl]
        ys.append(_bdot(p, pw_ref[gi]))
    y = jnp.concatenate(ys, axis=1) * ps_ref[...]
    o_ref[...] = x + gt_ref[...] * y
    u_ref[...] = u


def _pool_step(x, g, shift, scale, gate, buf_t, pool_w, pool_scale, pos0):
    bx, d = x.shape
    ng, gd, _ = pool_w.shape
    full = pl.BlockSpec((bx, d), lambda i: (0, 0))
    vec = pl.BlockSpec((1, d), lambda i: (0, 0))
    return pl.pallas_call(
        functools.partial(_pool_step_kernel, pos0=pos0),
        grid=(1,),
        in_specs=[full, vec, full, full, full,
                  pl.BlockSpec((POOL_BUF, bx, d), lambda i: (0, 0, 0)),
                  pl.BlockSpec((ng, gd, gd), lambda i: (0, 0, 0)), vec],
        out_specs=[full, full],
        out_shape=[jax.ShapeDtypeStruct((bx, d), F32), jax.ShapeDtypeStruct((bx, d), F32)],
        compiler_params=_cparams(1),
        name="pool_step",
    )(x, g.reshape(1, d), shift, scale, gate, buf_t, pool_w, pool_scale.reshape(1, d))


def _first_max(v, idx, n):
    m = jnp.max(v, axis=0, keepdims=True)
    first = jnp.min(jnp.where(v == m, idx, n), axis=0, keepdims=True)
    return jnp.where(idx == first, 1.0, 0.0), m, first


def _stack_rows(rows, dtype):
    n = rows[0].shape[1]
    r8 = lax.broadcasted_iota(I32, (8, n), 0)
    out = jnp.zeros((8, n), dtype)
    for k, r in enumerate(rows):
        out = jnp.where(r8 == k, jnp.broadcast_to(r.astype(dtype), (8, n)), out)
    return out


def _router_kernel(lg_ref, bias_ref, cin_ref, eid_ref, rank_ref, w_ref, cnt_ref, carry, *, tr):
    i = pl.program_id(0)

    @pl.when(i == 0)
    def _():
        carry[...] = cin_ref[:, 0:1]

    ne = lg_ref.shape[0]
    gs = ne // N_GROUPS
    s = jax.nn.sigmoid(lg_ref[...])
    biased = s + bias_ref[...]
    neg = -jnp.inf
    idx8 = lax.broadcasted_iota(I32, (gs, tr), 0)
    gidx = lax.broadcasted_iota(I32, (N_GROUPS, tr), 0)
    gsc = jnp.zeros((N_GROUPS, tr), F32)
    for g in range(N_GROUPS):
        blk = biased[g * gs:(g + 1) * gs, :]
        oh, m1, _ = _first_max(blk, idx8, gs)
        m2 = jnp.max(jnp.where(oh > 0, neg, blk), axis=0, keepdims=True)
        gsc = jnp.where(gidx == g, jnp.broadcast_to(m1 + m2, (N_GROUPS, tr)), gsc)
    gsel = jnp.zeros((N_GROUPS, tr), F32)
    for _ in range(TOPK_GROUPS):
        oh, _, _ = _first_max(gsc, gidx, N_GROUPS)
        gsel = gsel + oh
        gsc = jnp.where(oh > 0, neg, gsc)
    emask = jnp.concatenate([jnp.broadcast_to(gsel[g:g + 1, :], (gs, tr)) for g in range(N_GROUPS)], axis=0)
    masked = jnp.where(emask > 0, biased, neg)
    eidx = lax.broadcasted_iota(I32, (ne, tr), 0)
    ohs, ws, ids = [], [], []
    for _ in range(TOP_K):
        oh, _, first = _first_max(masked, eidx, ne)
        ohs.append(oh)
        ids.append(first)
        ws.append(jnp.sum(oh * s, axis=0, keepdims=True))
        masked = jnp.where(oh > 0, neg, masked)
    wsum = ws[0]
    cm = ohs[0]
    for k in range(1, TOP_K):
        wsum = wsum + ws[k]
        cm = cm + ohs[k]
    tri = (lax.broadcasted_iota(I32, (tr, tr), 0) < lax.broadcasted_iota(I32, (tr, tr), 1))
    prefix = jnp.dot(cm.astype(BF16), jnp.where(tri, 1.0, 0.0).astype(BF16),
                     preferred_element_type=F32) + carry[...]
    rk = [jnp.sum(ohs[k] * prefix, axis=0, keepdims=True) for k in range(TOP_K)]
    rank_ref[0] = _stack_rows(rk, F32).astype(I32)
    eid_ref[0] = _stack_rows(ids, I32)
    w_ref[...] = _stack_rows([w / wsum * ROUTED_SCALE for w in ws], F32)
    carry[...] = carry[...] + jnp.sum(cm, axis=1, keepdims=True)
    cnt_ref[...] = jnp.broadcast_to(carry[...], cnt_ref.shape)


def _router(logits_t, bias, counts_in, tr):
    ne, t = logits_t.shape
    nt = t // tr
    tile = pl.BlockSpec((1, 8, tr), lambda i: (i, 0, 0))
    return pl.pallas_call(
        functools.partial(_router_kernel, tr=tr),
        grid=(nt,),
        in_specs=[pl.BlockSpec((ne, tr), lambda i: (0, i)), pl.BlockSpec((ne, 1), lambda i: (0, 0)),
                  pl.BlockSpec((ne, LANE), lambda i: (0, 0))],
        out_specs=[tile, tile, pl.BlockSpec((8, tr), lambda i: (0, i)),
                   pl.BlockSpec((ne, LANE), lambda i: (0, 0))],
        out_shape=[jax.ShapeDtypeStruct((nt, 8, tr), I32), jax.ShapeDtypeStruct((nt, 8, tr), I32),
                   jax.ShapeDtypeStruct((8, t), F32), jax.ShapeDtypeStruct((ne, LANE), F32)],
        scratch_shapes=[pltpu.VMEM((ne, 1), F32)],
        compiler_params=_cparams(1),
        name="router",
    )(logits_t, bias.reshape(ne, 1), counts_in)


def _dispatch_kernel(zb_ref, ps_ref, eida_ref, ranka_ref, eidb_ref, rankb_ref, xa_ref, xb_ref, xs_ref,
                     zbuf, sem, zsem, *, nta, tm, ne):
    i = pl.program_id(0)

    @pl.when(i == 0)
    def _():
        zbuf[...] = jnp.zeros_like(zbuf)

        def zero_copy(e):
            return pltpu.make_async_copy(zbuf, xs_ref.at[pl.ds(zb_ref[e] * tm, tm), :], zsem)

        def zstart(e, c):
            @pl.when(zb_ref[e] >= 0)
            def _():
                zero_copy(e).start()
            return c

        def zwait(e, c):
            @pl.when(zb_ref[e] >= 0)
            def _():
                zero_copy(e).wait()
            return c

        lax.fori_loop(0, ne, zstart, 0)
        lax.fori_loop(0, ne, zwait, 0)

    def scatter(x_ref, eid_ref, rank_ref):
        def start(t, c):
            for k in range(TOP_K):
                row = ps_ref[eid_ref[0, k, t]] + rank_ref[0, k, t]
                pltpu.make_async_copy(x_ref.at[pl.ds(t, 1), :], xs_ref.at[pl.ds(row, 1), :],
                                      sem).start(priority=k % 2)
            return c

        def wait(t, c):
            for k in range(TOP_K):
                pltpu.make_async_copy(x_ref.at[pl.ds(0, 1), :], xs_ref.at[pl.ds(0, 1), :], sem).wait()
            return c

        lax.fori_loop(0, x_ref.shape[0], start, 0)
        lax.fori_loop(0, x_ref.shape[0], wait, 0)

    @pl.when(i < nta)
    def _():
        scatter(xa_ref, eida_ref, ranka_ref)

    @pl.when(i == nta)
    def _():
        scatter(xb_ref, eidb_ref, rankb_ref)


def _dispatch(zero_blk, pstart, eid_a, rank_a, packed_a, eid_b, rank_b, packed_b, r_max, tm):
    nta, _, tda = eid_a.shape
    _, _, tdb = eid_b.shape
    width = packed_a.shape[1]
    ne = zero_blk.shape[0]
    assert packed_a.shape[0] == nta * tda and packed_b.shape[0] == tdb and eid_b.shape[0] == 1
    ia = lambda i, zb, ps: (jnp.minimum(i, nta - 1), 0, 0)
    smem = lambda t, imap: pl.BlockSpec((1, 8, t), imap, memory_space=pltpu.SMEM)
    return pl.pallas_call(
        functools.partial(_dispatch_kernel, nta=nta, tm=tm, ne=ne),
        grid_spec=pltpu.PrefetchScalarGridSpec(
            num_scalar_prefetch=2,
            grid=(nta + 1,),
            in_specs=[smem(tda, ia), smem(tda, ia),
                      smem(tdb, lambda i, zb, ps: (0, 0, 0)), smem(tdb, lambda i, zb, ps: (0, 0, 0)),
                      pl.BlockSpec((tda, width), lambda i, zb, ps: (jnp.minimum(i, nta - 1), 0)),
                      pl.BlockSpec((tdb, width), lambda i, zb, ps: (0, 0))],
            out_specs=pl.BlockSpec(memory_space=pl.ANY),
            scratch_shapes=[pltpu.VMEM((tm, width), U32), pltpu.SemaphoreType.DMA, pltpu.SemaphoreType.DMA]),
        out_shape=jax.ShapeDtypeStruct((r_max, width), U32),
        compiler_params=_cparams(1),
        name="dispatch",
    )(zero_blk, pstart, eid_a, rank_a, eid_b, rank_b, packed_a, packed_b)


def _expert_kernel(be_ref, nr_ref, xb_ref, x_ref, wg_ref, wu_ref, wd_ref, wgt_ref, wut_ref, wdt_ref,
                   o_ref, xbf, *, nf_main, has_tail):
    del be_ref, xb_ref
    b = pl.program_id(0)
    f = pl.program_id(1)
    n = nr_ref[b]

    def mlp(wg, wu, wd):
        xs = xbf[...]
        g = jnp.dot(xs, wg[0, 0].astype(BF16), preferred_element_type=F32)
        u = jnp.dot(xs, wu[0, 0].astype(BF16), preferred_element_type=F32)
        h = (_silu(g) * u).astype(BF16)
        return jnp.dot(h, wd[0, 0].astype(BF16), preferred_element_type=F32)

    @pl.when((n > 0) & (f == 0))
    def _():
        lo, hi = _unpack_pair(x_ref[...])
        xbf[...] = jnp.concatenate([lo, hi], axis=1).astype(BF16)
        o_ref[...] = mlp(wg_ref, wu_ref, wd_ref)

    @pl.when((n > 0) & (f > 0) & (f < nf_main))
    def _():
        o_ref[...] += mlp(wg_ref, wu_ref, wd_ref)

    if has_tail:
        @pl.when((n > 0) & (f == nf_main))
        def _():
            o_ref[...] += mlp(wgt_ref, wut_ref, wdt_ref)


def _experts(layer, block_e, nrows, xblk, xs, w_gate, w_up, w_down, *, tm, n_blocks):
    _, ne, d, f = w_gate.shape
    width = xs.shape[1]
    tf = 256 if f >= 256 else f
    nf_main = f // tf
    tail = f - nf_main * tf
    has_tail = tail > 0
    nf = nf_main + (1 if has_tail else 0)
    tt = tail if has_tail else LANE
    tail_blk = (f - tt) // tt
    fm = lambda fi: jnp.minimum(fi, nf_main - 1)
    in_specs = [pl.BlockSpec((tm, width), lambda b, fi, be, nr, xb: (xb[b], 0)),
                pl.BlockSpec((1, 1, d, tf), lambda b, fi, be, nr, xb: (layer, be[b], 0, fm(fi))),
                pl.BlockSpec((1, 1, d, tf), lambda b, fi, be, nr, xb: (layer, be[b], 0, fm(fi))),
                pl.BlockSpec((1, 1, tf, d), lambda b, fi, be, nr, xb: (layer, be[b], fm(fi), 0)),
                pl.BlockSpec((1, 1, d, tt), lambda b, fi, be, nr, xb: (layer, be[b], 0, tail_blk)),
                pl.BlockSpec((1, 1, d, tt), lambda b, fi, be, nr, xb: (layer, be[b], 0, tail_blk)),
                pl.BlockSpec((1, 1, tt, d), lambda b, fi, be, nr, xb: (layer, be[b], tail_blk, 0))]
    return pl.pallas_call(
        functools.partial(_expert_kernel, nf_main=nf_main, has_tail=has_tail),
        grid_spec=pltpu.PrefetchScalarGridSpec(
            num_scalar_prefetch=3,
            grid=(n_blocks, nf),
            in_specs=in_specs,
            out_specs=pl.BlockSpec((tm, d), lambda b, fi, be, nr, xb: (xb[b], 0)),
            scratch_shapes=[pltpu.VMEM((tm, d), BF16)]),
        out_shape=jax.ShapeDtypeStruct((n_blocks * tm, d), F32),
        compiler_params=_cparams(2),
        name="experts",
    )(block_e, nrows, xblk, xs, w_gate, w_up, w_down, w_gate, w_up, w_down)


def _combine_kernel(ps_ref, eid_ref, rank_ref, x_ref, gt_ref, w_ref, ysh_ref, yb_ref, o_ref, gbuf, sem,
                    *, tmc, per_tile):
    off = (pl.program_id(1) % per_tile) * tmc

    def start(t, c):
        for k in range(TOP_K):
            row = ps_ref[eid_ref[0, k, off + t]] + rank_ref[0, k, off + t]
            pltpu.make_async_copy(yb_ref.at[pl.ds(row, 1), :], gbuf.at[k, pl.ds(t, 1), :],
                                  sem).start(priority=k % 2)
        return c

    def wait(t, c):
        for k in range(TOP_K):
            pltpu.make_async_copy(yb_ref.at[pl.ds(0, 1), :], gbuf.at[0, pl.ds(0, 1), :], sem).wait()
        return c

    lax.fori_loop(0, tmc, start, 0)
    lax.fori_loop(0, tmc, wait, 0)
    w = w_ref[...]
    acc = w[:, 0:1] * gbuf[0]
    for k in range(1, TOP_K):
        acc = acc + w[:, k:k + 1] * gbuf[k]
    o_ref[0] = x_ref[0] + gt_ref[0] * (acc + ysh_ref[...])


def _combine(pstart, x, gate, eid, rank, w_tok, ysh, yb):
    bx, l, d = x.shape
    tr = eid.shape[2]
    tmc = _pick(tr, (256, 128, 64, 32, 16))
    per_tile = tr // tmc
    nl = l // tmc
    assert l % tr == 0
    gspec = (pl.BlockSpec((1, 1, d), lambda b, i, ps: (b, 0, 0)) if gate.shape[1] == 1
             else pl.BlockSpec((1, tmc, d), lambda b, i, ps: (b, i, 0)))
    smem = pl.BlockSpec((1, 8, tr), lambda b, i, ps: ((b * nl + i) // per_tile, 0, 0), memory_space=pltpu.SMEM)
    return pl.pallas_call(
        functools.partial(_combine_kernel, tmc=tmc, per_tile=per_tile),
        grid_spec=pltpu.PrefetchScalarGridSpec(
            num_scalar_prefetch=1,
            grid=(bx, nl),
            in_specs=[smem, smem,
                      pl.BlockSpec((1, tmc, d), lambda b, i, ps: (b, i, 0)),
                      gspec,
                      pl.BlockSpec((tmc, 8), lambda b, i, ps: (b * nl + i, 0)),
                      pl.BlockSpec((tmc, d), lambda b, i, ps: (b * nl + i, 0)),
                      pl.BlockSpec(memory_space=pl.ANY)],
            out_specs=pl.BlockSpec((1, tmc, d), lambda b, i, ps: (b, i, 0)),
            scratch_shapes=[pltpu.VMEM((TOP_K, tmc, d), F32), pltpu.SemaphoreType.DMA]),
        out_shape=jax.ShapeDtypeStruct((bx, l, d), F32),
        compiler_params=_cparams(2),
        name="combine",
    )(pstart, eid, rank, x, gate, w_tok, ysh, yb)


def _expert_block_rows(t_all, ne):
    avg = max(1, t_all * TOP_K // ne)
    return 896 if avg >= 512 else max(32, 2 << avg.bit_length())


def _moe(layer, xa, xb, mods_a, mods_b, norm_g, router_w, router_bias, w_gate, w_up, w_down, ws_gate, ws_up, ws_down):
    _, ne, d, _ = w_gate.shape
    ta, tb = xa.shape[0] * xa.shape[1], xb.shape[1]
    t_all = ta + tb
    tm = _expert_block_rows(t_all, ne)
    n_blocks = (t_all * TOP_K + ne * (tm - 1)) // tm

    router_wt = router_w.T
    packed_a, lg_a = _norm_ffn(xa, norm_g, mods_a[0], mods_a[1], router_wt)
    packed_b, lg_b = _norm_ffn(xb, norm_g, mods_b[0], mods_b[1], router_wt)
    tra = _pick(xa.shape[1], (512, 256, 128, 64, 32, 16))
    eid_a, rank_a, w_a, cnt_a = _router(lg_a, router_bias, jnp.zeros((ne, LANE), F32), tra)
    eid_b, rank_b, w_b, cnt = _router(lg_b, router_bias, cnt_a, tb)

    counts = cnt[:, 0].astype(I32)
    nblk = (counts + tm - 1) // tm
    bend = jnp.cumsum(nblk)
    bstart = bend - nblk
    n_used = bend[-1]
    bidx = jnp.arange(n_blocks, dtype=I32)
    last = jnp.maximum(n_used - 1, 0)
    cb = jnp.minimum(bidx, last)
    block_e = jnp.minimum(jnp.sum((bend[None, :] <= cb[:, None]).astype(I32), axis=1), ne - 1)
    nrows = jnp.where(bidx < n_used, 1, 0).astype(I32)
    zero_blk = jnp.where(counts > 0, bend - 1, -1).astype(I32)
    pstart = (bstart * tm).astype(I32)

    xs_sorted = _dispatch(zero_blk, pstart, eid_a, rank_a, packed_a, eid_b, rank_b, packed_b, n_blocks * tm, tm)
    yb = _experts(layer, block_e, nrows, cb, xs_sorted, w_gate, w_up, w_down, tm=tm, n_blocks=n_blocks)

    outs = []
    for x, packed, mods, eid, rank, w in ((xa, packed_a, mods_a, eid_a, rank_a, w_a),
                                          (xb, packed_b, mods_b, eid_b, rank_b, w_b)):
        nt = packed.shape[0]
        tms = _pick(nt, (1024, 512, 256, 128, 64, 32, 16))
        nsb = nt // tms
        ysh = _experts(layer, jnp.zeros((nsb,), I32), jnp.ones((nsb,), I32), jnp.arange(nsb, dtype=I32), packed,
                       ws_gate[:, None], ws_up[:, None], ws_down[:, None], tm=tms, n_blocks=nsb)
        outs.append(_combine(pstart, x, mods[2], eid, rank, w.T, ysh, yb))
    return outs


def kernel(x_prompt, x_sample, state_lru_h, state_lru_conv, state_pool, c_prompt, c_sample, ada_w, ada_b, norm_mix, norm_ffn, norm_out, lru_w_in, lru_conv_w, lru_conv_b, lru_w_rg, lru_b_rg, lru_w_ig, lru_b_ig, lru_lambda, lru_w_out, pool_w, pool_scale, router_w, router_bias, exp_w_gate, exp_w_up, exp_w_down, shared_w_gate, shared_w_up, shared_w_down):
    bp, seq, d = x_prompt.shape
    bs = x_sample.shape[0]
    depth = ada_w.shape[0]
    ch = lru_w_in.shape[2] // 2
    assert x_sample.shape[1] == 1 and seq >= 16

    bc = -(-(bp + bs) // 8) * 8
    c_all = jnp.concatenate([c_prompt, c_sample, jnp.zeros((bc - bp - bs, d), F32)], axis=0)
    mod = _ada(c_all, ada_w, ada_b)

    def mods(i):
        mp = [mod[i, :bp, k * d:(k + 1) * d].reshape(bp, 1, d) for k in range(6)]
        ms = [mod[i, bp:bp + bs, k * d:(k + 1) * d].reshape(1, bs, d) for k in range(6)]
        return mp, ms

    xp = x_prompt
    xs = x_sample.reshape(1, bs, d)
    tn_in = _pick(ch, (896, 640, 512, 384, 256, 128))
    tn_out = _pick(d, (512, 256, 128))
    new_h_p, new_cv_p, new_pb_p, new_h_s, new_cv_s, new_pb_s = [], [], [], [], [], []
    for i in range(depth):
        mp, ms = mods(i)
        j = i // 2
        if i % 2 == 0:
            w_in = lru_w_in[j]
            lru = (lru_conv_w[j], lru_conv_b[j], lru_w_rg[j], lru_w_ig[j], lru_b_rg[j], lru_b_ig[j], lru_lambda[j])
            u = _norm_mix(xp, norm_mix[i], mp[0], mp[1])
            gate = _mm(u, w_in, col0=0, ncols=ch, tn=tn_in, act="gelu", name="lru_in_gate")
            rec = _mm(u, w_in, col0=ch, ncols=ch, tn=tn_in, name="lru_in_rec")
            ypre, h_last = _lru_seq(rec, gate, jnp.zeros((bp, ch), F32), jnp.zeros((bp, 3, ch), F32), *lru)
            xp = _mm(ypre, lru_w_out[j], tn=tn_out, resid=xp, gate=mp[2], name="lru_out")
            new_h_p.append(h_last.reshape(bp, ch))
            new_cv_p.append(rec[:, seq - 3:, :])
            u = _norm_mix(xs, norm_mix[i], ms[0], ms[1])
            gate = _mm(u, w_in, col0=0, ncols=ch, tn=tn_in, act="gelu", name="lru_in_gate")
            rec = _mm(u, w_in, col0=ch, ncols=ch, tn=tn_in, name="lru_in_rec")
            cv0 = state_lru_conv[j]
            ypre, h_new = _lru_step(rec[0], gate[0], state_lru_h[j], cv0.transpose(1, 0, 2), *lru)
            xs = _mm(ypre[None], lru_w_out[j], tn=tn_out, resid=xs, gate=ms[2], name="lru_out")
            new_h_s.append(h_new)
            new_cv_s.append(jnp.concatenate([cv0[:, 1:, :], rec[0][:, None, :]], axis=1))
        else:
            xp, pb = _pool_seq(xp, norm_mix[i], mp[0], mp[1], mp[2], jnp.zeros((bp, POOL_BUF, d), F32),
                               pool_w[j], pool_scale[j], 0)
            new_pb_p.append(pb)
            buf0 = state_pool[j]
            xs2, u_s = _pool_step(xs[0], norm_mix[i], ms[0][0], ms[1][0], ms[2][0], buf0.transpose(1, 0, 2),
                                  pool_w[j], pool_scale[j], PAST_LEN)
            xs = xs2[None]
            new_pb_s.append(jnp.concatenate([buf0[:, 1:, :], u_s[:, None, :]], axis=1))
        xp, xs = _moe(i, xp, xs, (mp[3], mp[4], mp[5]), (ms[3], ms[4], ms[5]), norm_ffn[i],
                      router_w[i], router_bias[i], exp_w_gate, exp_w_up, exp_w_down,
                      shared_w_gate, shared_w_up, shared_w_down)
    y_prompt = _norm_out(xp, norm_out)
    y_sample = _norm_out(xs, norm_out).reshape(bs, 1, d)
    return (y_prompt, y_sample, jnp.stack(new_h_p), jnp.stack(new_cv_p), jnp.stack(new_pb_p),
            jnp.stack(new_h_s), jnp.stack(new_cv_s), jnp.stack(new_pb_s))
```

```python
import functools

import jax
import jax.numpy as jnp
from jax import lax
from jax.experimental import pallas as pl
from jax.experimental.pallas import tpu as pltpu

F32 = jnp.float32
BF16 = jnp.bfloat16
I32 = jnp.int32
U32 = jnp.uint32

EPS = 1e-6
LRU_C = 8.0
LRU_BLOCK = 128
POOL_WINDOWS = (2, 4, 8, 16)
POOL_BUF = max(POOL_WINDOWS) - 1
TOP_K = 6
N_GROUPS = 8
TOPK_GROUPS = 4
ROUTED_SCALE = 2.5
PAST_LEN = 16384

V7X_VMEM_LIMIT_BYTES = 58 * 1024 * 1024
LANE = 128


def _cparams(n_axes):
    return pltpu.CompilerParams(dimension_semantics=("arbitrary",) * n_axes,
                                vmem_limit_bytes=V7X_VMEM_LIMIT_BYTES)


def _pick(n, candidates):
    for c in candidates:
        if c <= n and n % c == 0:
            return c
    return n


def _bdot(a, b):
    return jnp.dot(a.astype(BF16), b.astype(BF16), preferred_element_type=F32)


def _silu(x):
    return x * jax.nn.sigmoid(x)


def _ada_kernel(c_ref, w_ref, b_ref, o_ref):
    s = _silu(c_ref[...])
    o_ref[0] = _bdot(s, w_ref[0]) + b_ref[0]


def _ada(c_all, ada_w, ada_b):
    depth, d, n = ada_w.shape
    bc = c_all.shape[0]
    tn = _pick(n, (1024, 512, 256, 128))
    return pl.pallas_call(
        _ada_kernel,
        grid=(depth, n // tn),
        in_specs=[pl.BlockSpec((bc, d), lambda l, j: (0, 0)),
                  pl.BlockSpec((1, d, tn), lambda l, j: (l, 0, j)),
                  pl.BlockSpec((1, 1, tn), lambda l, j: (l, 0, j))],
        out_specs=pl.BlockSpec((1, bc, tn), lambda l, j: (l, 0, j)),
        out_shape=jax.ShapeDtypeStruct((depth, bc, n), F32),
        compiler_params=_cparams(2),
        name="ada_mod",
    )(c_all, ada_w, ada_b.reshape(depth, 1, n))


def _rms(x, g):
    return x * lax.rsqrt(jnp.mean(x * x, axis=-1, keepdims=True) + EPS) * g


def _modnorm(x, g, shift, scale):
    return _rms(x, g) * (1.0 + scale) + shift


def _mod_spec(mod, tm):
    d = mod.shape[-1]
    if mod.shape[1] == 1:
        return pl.BlockSpec((1, 1, d), lambda b, i: (b, 0, 0))
    return pl.BlockSpec((1, tm, d), lambda b, i: (b, i, 0))


def _norm_mix_kernel(x_ref, g_ref, sh_ref, sc_ref, o_ref):
    o_ref[0] = _modnorm(x_ref[0], g_ref[...], sh_ref[0], sc_ref[0]).astype(o_ref.dtype)


def _norm_mix(x, g, shift, scale):
    bx, l, d = x.shape
    tm = _pick(l, (512, 256, 128, 64, 32, 16))
    return pl.pallas_call(
        _norm_mix_kernel,
        grid=(bx, l // tm),
        in_specs=[pl.BlockSpec((1, tm, d), lambda b, i: (b, i, 0)),
                  pl.BlockSpec((1, d), lambda b, i: (0, 0)),
                  _mod_spec(shift, tm), _mod_spec(scale, tm)],
        out_specs=pl.BlockSpec((1, tm, d), lambda b, i: (b, i, 0)),
        out_shape=jax.ShapeDtypeStruct((bx, l, d), BF16),
        compiler_params=_cparams(2),
        name="norm_mix",
    )(x, g.reshape(1, d), shift, scale)


def _norm_out_kernel(x_ref, g_ref, o_ref):
    o_ref[0] = _rms(x_ref[0], g_ref[...])


def _norm_out(x, g):
    bx, l, d = x.shape
    tm = _pick(l, (512, 256, 128, 64, 32, 16, 8))
    return pl.pallas_call(
        _norm_out_kernel,
        grid=(bx, l // tm),
        in_specs=[pl.BlockSpec((1, tm, d), lambda b, i: (b, i, 0)),
                  pl.BlockSpec((1, d), lambda b, i: (0, 0))],
        out_specs=pl.BlockSpec((1, tm, d), lambda b, i: (b, i, 0)),
        out_shape=jax.ShapeDtypeStruct((bx, l, d), F32),
        compiler_params=_cparams(2),
        name="norm_out",
    )(x, g.reshape(1, d))


def _pack_pair(lo, hi):
    lo_b = lax.bitcast_convert_type(lo.astype(BF16).astype(F32), U32)
    hi_b = lax.bitcast_convert_type(hi.astype(BF16).astype(F32), U32)
    return lax.shift_right_logical(lo_b, jnp.uint32(16)) | (hi_b & jnp.uint32(0xFFFF0000))


def _unpack_pair(word):
    lo = lax.bitcast_convert_type(lax.shift_left(word, jnp.uint32(16)), F32)
    hi = lax.bitcast_convert_type(word & jnp.uint32(0xFFFF0000), F32)
    return lo, hi


def _norm_ffn_kernel(x_ref, g_ref, sh_ref, sc_ref, rwt_ref, up_ref, lg_ref):
    u = _modnorm(x_ref[0], g_ref[...], sh_ref[0], sc_ref[0])
    half = u.shape[1] // 2
    up_ref[...] = _pack_pair(u[:, :half], u[:, half:])
    lg_ref[...] = lax.dot_general(rwt_ref[...], u, (((1,), (1,)), ((), ())),
                                  precision=lax.Precision.HIGHEST, preferred_element_type=F32)


def _norm_ffn(x, g, shift, scale, router_wt):
    bx, l, d = x.shape
    e = router_wt.shape[0]
    tm = _pick(l, (512, 256, 128))
    nl = l // tm
    return pl.pallas_call(
        _norm_ffn_kernel,
        grid=(bx, nl),
        in_specs=[pl.BlockSpec((1, tm, d), lambda b, i: (b, i, 0)),
                  pl.BlockSpec((1, d), lambda b, i: (0, 0)),
                  _mod_spec(shift, tm), _mod_spec(scale, tm),
                  pl.BlockSpec((e, d), lambda b, i: (0, 0))],
        out_specs=[pl.BlockSpec((tm, d // 2), lambda b, i: (b * nl + i, 0)),
                   pl.BlockSpec((e, tm), lambda b, i: (0, b * nl + i))],
        out_shape=[jax.ShapeDtypeStruct((bx * l, d // 2), U32),
                   jax.ShapeDtypeStruct((e, bx * l), F32)],
        compiler_params=_cparams(2),
        name="norm_ffn",
    )(x, g.reshape(1, d), shift, scale, router_wt)


def _mm_kernel(*refs, act, resid):
    if resid:
        x_ref, w_ref, r_ref, g_ref, o_ref, wbf = refs
    else:
        x_ref, w_ref, o_ref, wbf = refs

    @pl.when((pl.program_id(1) == 0) & (pl.program_id(2) == 0))
    def _():
        wbf[...] = w_ref[...].astype(BF16)

    acc = jnp.dot(x_ref[0], wbf[...], preferred_element_type=F32)
    if act == "gelu":
        acc = jax.nn.gelu(acc)
    if resid:
        acc = r_ref[0] + g_ref[0] * acc
    o_ref[0] = acc.astype(o_ref.dtype)


def _mm(x, w, *, col0=0, ncols=None, tn, act=None, resid=None, gate=None, out_dtype=F32, name="mm"):
    bx, l, k = x.shape
    ncols = w.shape[1] - col0 if ncols is None else ncols
    tm = _pick(l, (512, 256, 128, 64, 32, 16))
    nj = ncols // tn
    j0 = col0 // tn
    in_specs = [pl.BlockSpec((1, tm, k), lambda j, b, i: (b, i, 0)),
                pl.BlockSpec((k, tn), lambda j, b, i: (0, j0 + j))]
    args = [x, w]
    if resid is not None:
        in_specs.append(pl.BlockSpec((1, tm, tn), lambda j, b, i: (b, i, j)))
        if gate.shape[1] == 1:
            in_specs.append(pl.BlockSpec((1, 1, tn), lambda j, b, i: (b, 0, j)))
        else:
            in_specs.append(pl.BlockSpec((1, tm, tn), lambda j, b, i: (b, i, j)))
        args += [resid, gate]
    return pl.pallas_call(
        functools.partial(_mm_kernel, act=act, resid=resid is not None),
        grid=(nj, bx, l // tm),
        in_specs=in_specs,
        out_specs=pl.BlockSpec((1, tm, tn), lambda j, b, i: (b, i, j)),
        out_shape=jax.ShapeDtypeStruct((bx, l, ncols), out_dtype),
        scratch_shapes=[pltpu.VMEM((k, tn), BF16)],
        compiler_params=_cparams(3),
        name=name,
    )(*args)


def _lru_coeffs(xc, wrg, wig, brg, big, lam):
    heads = xc.shape[1] // LRU_BLOCK
    rs, igs = [], []
    for h in range(heads):
        xh = xc[:, h * LRU_BLOCK:(h + 1) * LRU_BLOCK].astype(BF16)
        rs.append(jnp.dot(xh, wrg[h], preferred_element_type=F32))
        igs.append(jnp.dot(xh, wig[h], preferred_element_type=F32))
    r = jax.nn.sigmoid(jnp.concatenate(rs, axis=1) + brg)
    ig = jax.nn.sigmoid(jnp.concatenate(igs, axis=1) + big)
    log_a = -LRU_C * r * jax.nn.softplus(-lam)
    a = jnp.exp(log_a)
    b = jnp.sqrt(1.0 - jnp.exp(2.0 * log_a)) * (ig * xc)
    return a, b


def _lru_seq_kernel(rec_ref, gate_ref, h0_ref, cv0_ref, cw_ref, cb_ref, wrg_ref, wig_ref, brg_ref,
                    big_ref, lam_ref, y_ref, hout_ref, ebuf, hcar, wrgb, wigb, *, tc):
    c = pl.program_id(2)

    @pl.when(c == 0)
    def _():
        ebuf[5:8, :] = cv0_ref[0]
        hcar[...] = h0_ref[0]
        wrgb[...] = wrg_ref[...].astype(BF16)
        wigb[...] = wig_ref[...].astype(BF16)

    rec = rec_ref[0]
    ebuf[8:8 + tc, :] = rec
    cw = cw_ref[...]
    xc = cb_ref[...] + ebuf[5:5 + tc, :] * cw[0:1] + ebuf[6:6 + tc, :] * cw[1:2] \
        + ebuf[7:7 + tc, :] * cw[2:3] + rec * cw[3:4]
    a, b = _lru_coeffs(xc, wrgb, wigb, brg_ref[...], big_ref[...], lam_ref[...])

    row = lax.broadcasted_iota(I32, a.shape, 0)
    s = 1
    while s < tc:
        m = row >= s
        a_sh = pltpu.roll(a, s, 0)
        b_sh = pltpu.roll(b, s, 0)
        b = jnp.where(m, a * b_sh + b, b)
        a = jnp.where(m, a * a_sh, a)
        s *= 2
    h = a * hcar[...] + b
    y_ref[0] = (gate_ref[0] * h).astype(y_ref.dtype)
    hlast = h[tc - 1:tc, :]
    hcar[...] = hlast
    hout_ref[0] = hlast
    ebuf[5:8, :] = rec[tc - 3:tc, :]


def _lru_seq(rec, gate, h0, cv0, conv_w, conv_b, w_rg, w_ig, b_rg, b_ig, lam):
    bx, l, ch = rec.shape
    tc = _pick(l, (256, 128, 64, 32, 16, 8))
    heads = ch // LRU_BLOCK
    hpt = 7 if heads % 7 == 0 else (5 if heads % 5 == 0 else 1)
    ct = hpt * LRU_BLOCK
    nct = ch // ct
    row = lambda a: a.reshape(1, ch)
    cspec = lambda r: pl.BlockSpec((r, ct), lambda b, j, c: (0, j))
    return pl.pallas_call(
        functools.partial(_lru_seq_kernel, tc=tc),
        grid=(bx, nct, l // tc),
        in_specs=[pl.BlockSpec((1, tc, ct), lambda b, j, c: (b, c, j)),
                  pl.BlockSpec((1, tc, ct), lambda b, j, c: (b, c, j)),
                  pl.BlockSpec((1, 1, ct), lambda b, j, c: (b, 0, j)),
                  pl.BlockSpec((1, 3, ct), lambda b, j, c: (b, 0, j)),
                  cspec(4), cspec(1),
                  pl.BlockSpec((hpt, LRU_BLOCK, LRU_BLOCK), lambda b, j, c: (j, 0, 0)),
                  pl.BlockSpec((hpt, LRU_BLOCK, LRU_BLOCK), lambda b, j, c: (j, 0, 0)),
                  cspec(1), cspec(1), cspec(1)],
        out_specs=[pl.BlockSpec((1, tc, ct), lambda b, j, c: (b, c, j)),
                   pl.BlockSpec((1, 1, ct), lambda b, j, c: (b, 0, j))],
        out_shape=[jax.ShapeDtypeStruct((bx, l, ch), BF16),
                   jax.ShapeDtypeStruct((bx, 1, ch), F32)],
        scratch_shapes=[pltpu.VMEM((8 + tc, ct), F32), pltpu.VMEM((1, ct), F32),
                        pltpu.VMEM((hpt, LRU_BLOCK, LRU_BLOCK), BF16),
                        pltpu.VMEM((hpt, LRU_BLOCK, LRU_BLOCK), BF16)],
        compiler_params=_cparams(3),
        name="lru_seq",
    )(rec, gate, h0.reshape(bx, 1, ch), cv0, conv_w, row(conv_b), w_rg, w_ig, row(b_rg), row(b_ig), row(lam))


def _lru_step_kernel(rec_ref, gate_ref, h0_ref, cv0_ref, cw_ref, cb_ref, wrg_ref, wig_ref, brg_ref,
                     big_ref, lam_ref, y_ref, hout_ref):
    rec = rec_ref[...]
    cw = cw_ref[...]
    xc = cb_ref[...] + cv0_ref[0] * cw[0:1] + cv0_ref[1] * cw[1:2] + cv0_ref[2] * cw[2:3] + rec * cw[3:4]
    a, b = _lru_coeffs(xc, wrg_ref[...].astype(BF16), wig_ref[...].astype(BF16),
                       brg_ref[...], big_ref[...], lam_ref[...])
    h = a * h0_ref[...] + b
    y_ref[...] = (gate_ref[...] * h).astype(y_ref.dtype)
    hout_ref[...] = h


def _lru_step(rec, gate, h0, cv0_t, conv_w, conv_b, w_rg, w_ig, b_rg, b_ig, lam):
    bx, ch = rec.shape
    heads = ch // LRU_BLOCK
    hpt = 7 if heads % 7 == 0 else (5 if heads % 5 == 0 else 1)
    ct = hpt * LRU_BLOCK
    row = lambda a: a.reshape(1, ch)
    bspec = pl.BlockSpec((bx, ct), lambda j: (0, j))
    cspec = lambda r: pl.BlockSpec((r, ct), lambda j: (0, j))
    wspec = pl.BlockSpec((hpt, LRU_BLOCK, LRU_BLOCK), lambda j: (j, 0, 0))
    return pl.pallas_call(
        _lru_step_kernel,
        grid=(ch // ct,),
        in_specs=[bspec, bspec, bspec, pl.BlockSpec((3, bx, ct), lambda j: (0, 0, j)),
                  cspec(4), cspec(1), wspec, wspec, cspec(1), cspec(1), cspec(1)],
        out_specs=[bspec, bspec],
        out_shape=[jax.ShapeDtypeStruct((bx, ch), BF16), jax.ShapeDtypeStruct((bx, ch), F32)],
        compiler_params=_cparams(1),
        name="lru_step",
    )(rec, gate, h0, cv0_t, conv_w, row(conv_b), w_rg, w_ig, row(b_rg), row(b_ig), row(lam))


def _pool_seq_kernel(x_ref, g_ref, sh_ref, sc_ref, gt_ref, buf0_ref, pw_ref, ps_ref, o_ref, st_ref,
                     ebuf, pwb, *, tc, pos0):
    c = pl.program_id(1)
    d = x_ref.shape[2]
    gd = d // len(POOL_WINDOWS)

    @pl.when(c == 0)
    def _():
        ebuf[1:16, :] = buf0_ref[0]
        pwb[...] = pw_ref[...].astype(BF16)

    x = x_ref[0]
    u = _modnorm(x, g_ref[...], sh_ref[0], sc_ref[0])
    ebuf[16:16 + tc, :] = u
    pos = pos0 + c * tc + lax.broadcasted_iota(I32, (tc, gd), 0)
    ys = []
    for gi, w in enumerate(POOL_WINDOWS):
        sl = slice(gi * gd, (gi + 1) * gd)
        ssum = u[:, sl]
        for j in range(1, w):
            ssum = ssum + ebuf[16 - j:16 - j + tc, sl]
        cnt = jnp.minimum(w, pos + 1).astype(F32)
        p = ssum / cnt - u[:, sl]
        ys.append(jnp.dot(p.astype(BF16), pwb[gi], preferred_element_type=F32))
    y = jnp.concatenate(ys, axis=1) * ps_ref[...]
    o_ref[0] = x + gt_ref[0] * y
    st_ref[0] = ebuf[tc + 1:tc + 16, :]
    ebuf[0:16, :] = ebuf[tc:tc + 16, :]


def _pool_seq(x, g, shift, scale, gate, buf0, pool_w, pool_scale, pos0):
    bx, l, d = x.shape
    tc = _pick(l, (256, 128, 64, 32, 16))
    ng, gd, _ = pool_w.shape
    mspec = lambda m: (pl.BlockSpec((1, 1, d), lambda b, c: (b, 0, 0)) if m.shape[1] == 1
                       else pl.BlockSpec((1, tc, d), lambda b, c: (b, c, 0)))
    return pl.pallas_call(
        functools.partial(_pool_seq_kernel, tc=tc, pos0=pos0),
        grid=(bx, l // tc),
        in_specs=[pl.BlockSpec((1, tc, d), lambda b, c: (b, c, 0)),
                  pl.BlockSpec((1, d), lambda b, c: (0, 0)),
                  mspec(shift), mspec(scale), mspec(gate),
                  pl.BlockSpec((1, POOL_BUF, d), lambda b, c: (b, 0, 0)),
                  pl.BlockSpec((ng, gd, gd), lambda b, c: (0, 0, 0)),
                  pl.BlockSpec((1, d), lambda b, c: (0, 0))],
        out_specs=[pl.BlockSpec((1, tc, d), lambda b, c: (b, c, 0)),
                   pl.BlockSpec((1, POOL_BUF, d), lambda b, c: (b, 0, 0))],
        out_shape=[jax.ShapeDtypeStruct((bx, l, d), F32),
                   jax.ShapeDtypeStruct((bx, POOL_BUF, d), F32)],
        scratch_shapes=[pltpu.VMEM((16 + tc, d), F32), pltpu.VMEM((ng, gd, gd), BF16)],
        compiler_params=_cparams(2),
        name="pool_seq",
    )(x, g.reshape(1, d), shift, scale, gate, buf0, pool_w, pool_scale.reshape(1, d))


def _pool_step_kernel(x_ref, g_ref, sh_ref, sc_ref, gt_ref, buf_ref, pw_ref, ps_ref, o_ref, u_ref, *, pos0):
    d = x_ref.shape[1]
    gd = d // len(POOL_WINDOWS)
    x = x_ref[...]
    u = _modnorm(x, g_ref[...], sh_ref[...], sc_ref[...])
    ys = []
    for gi, w in enumerate(POOL_WINDOWS):
        sl = slice(gi * gd, (gi + 1) * gd)
        ssum = u[:, sl]
        for j in range(1, w):
            ssum = ssum + buf_ref[POOL_BUF - j, :, sl]
        cnt = float(min(w, pos0 + 1))
        p = ssum / cnt - u[:, sl]
        ys.append(_bdot(p, pw_ref[gi]))
    y = jnp.concatenate(ys, axis=1) * ps_ref[...]
    o_ref[...] = x + gt_ref[...] * y
    u_ref[...] = u


def _pool_step(x, g, shift, scale, gate, buf_t, pool_w, pool_scale, pos0):
    bx, d = x.shape
    ng, gd, _ = pool_w.shape
    full = pl.BlockSpec((bx, d), lambda i: (0, 0))
    vec = pl.BlockSpec((1, d), lambda i: (0, 0))
    return pl.pallas_call(
        functools.partial(_pool_step_kernel, pos0=pos0),
        grid=(1,),
        in_specs=[full, vec, full, full, full,
                  pl.BlockSpec((POOL_BUF, bx, d), lambda i: (0, 0, 0)),
                  pl.BlockSpec((ng, gd, gd), lambda i: (0, 0, 0)), vec],
        out_specs=[full, full],
        out_shape=[jax.ShapeDtypeStruct((bx, d), F32), jax.ShapeDtypeStruct((bx, d), F32)],
        compiler_params=_cparams(1),
        name="pool_step",
    )(x, g.reshape(1, d), shift, scale, gate, buf_t, pool_w, pool_scale.reshape(1, d))


def _first_max(v, idx, n):
    m = jnp.max(v, axis=0, keepdims=True)
    first = jnp.min(jnp.where(v == m, idx, n), axis=0, keepdims=True)
    return jnp.where(idx == first, 1.0, 0.0), m, first


def _stack_rows(rows, dtype):
    n = rows[0].shape[1]
    r8 = lax.broadcasted_iota(I32, (8, n), 0)
    out = jnp.zeros((8, n), dtype)
    for k, r in enumerate(rows):
        out = jnp.where(r8 == k, jnp.broadcast_to(r.astype(dtype), (8, n)), out)
    return out


def _router_kernel(lg_ref, bias_ref, cin_ref, eid_ref, rank_ref, w_ref, cnt_ref, carry, *, tr):
    i = pl.program_id(0)

    @pl.when(i == 0)
    def _():
        carry[...] = cin_ref[:, 0:1]

    ne = lg_ref.shape[0]
    gs = ne // N_GROUPS
    s = jax.nn.sigmoid(lg_ref[...])
    biased = s + bias_ref[...]
    neg = -jnp.inf
    idx8 = lax.broadcasted_iota(I32, (gs, tr), 0)
    gidx = lax.broadcasted_iota(I32, (N_GROUPS, tr), 0)
    gsc = jnp.zeros((N_GROUPS, tr), F32)
    for g in range(N_GROUPS):
        blk = biased[g * gs:(g + 1) * gs, :]
        oh, m1, _ = _first_max(blk, idx8, gs)
        m2 = jnp.max(jnp.where(oh > 0, neg, blk), axis=0, keepdims=True)
        gsc = jnp.where(gidx == g, jnp.broadcast_to(m1 + m2, (N_GROUPS, tr)), gsc)
    gsel = jnp.zeros((N_GROUPS, tr), F32)
    for _ in range(TOPK_GROUPS):
        oh, _, _ = _first_max(gsc, gidx, N_GROUPS)
        gsel = gsel + oh
        gsc = jnp.where(oh > 0, neg, gsc)
    emask = jnp.concatenate([jnp.broadcast_to(gsel[g:g + 1, :], (gs, tr)) for g in range(N_GROUPS)], axis=0)
    masked = jnp.where(emask > 0, biased, neg)
    eidx = lax.broadcasted_iota(I32, (ne, tr), 0)
    ohs, ws, ids = [], [], []
    for _ in range(TOP_K):
        oh, _, first = _first_max(masked, eidx, ne)
        ohs.append(oh)
        ids.append(first)
        ws.append(jnp.sum(oh * s, axis=0, keepdims=True))
        masked = jnp.where(oh > 0, neg, masked)
    wsum = ws[0]
    cm = ohs[0]
    for k in range(1, TOP_K):
        wsum = wsum + ws[k]
        cm = cm + ohs[k]
    tri = (lax.broadcasted_iota(I32, (tr, tr), 0) < lax.broadcasted_iota(I32, (tr, tr), 1))
    prefix = jnp.dot(cm.astype(BF16), jnp.where(tri, 1.0, 0.0).astype(BF16),
                     preferred_element_type=F32) + carry[...]
    rk = [jnp.sum(ohs[k] * prefix, axis=0, keepdims=True) for k in range(TOP_K)]
    rank_ref[0] = _stack_rows(rk, F32).astype(I32)
    eid_ref[0] = _stack_rows(ids, I32)
    w_ref[...] = _stack_rows([w / wsum * ROUTED_SCALE for w in ws], F32)
    carry[...] = carry[...] + jnp.sum(cm, axis=1, keepdims=True)
    cnt_ref[...] = jnp.broadcast_to(carry[...], cnt_ref.shape)


def _router(logits_t, bias, counts_in, tr):
    ne, t = logits_t.shape
    nt = t // tr
    tile = pl.BlockSpec((1, 8, tr), lambda i: (i, 0, 0))
    return pl.pallas_call(
        functools.partial(_router_kernel, tr=tr),
        grid=(nt,),
        in_specs=[pl.BlockSpec((ne, tr), lambda i: (0, i)), pl.BlockSpec((ne, 1), lambda i: (0, 0)),
                  pl.BlockSpec((ne, LANE), lambda i: (0, 0))],
        out_specs=[tile, tile, pl.BlockSpec((8, tr), lambda i: (0, i)),
                   pl.BlockSpec((ne, LANE), lambda i: (0, 0))],
        out_shape=[jax.ShapeDtypeStruct((nt, 8, tr), I32), jax.ShapeDtypeStruct((nt, 8, tr), I32),
                   jax.ShapeDtypeStruct((8, t), F32), jax.ShapeDtypeStruct((ne, LANE), F32)],
        scratch_shapes=[pltpu.VMEM((ne, 1), F32)],
        compiler_params=_cparams(1),
        name="router",
    )(logits_t, bias.reshape(ne, 1), counts_in)


def _dispatch_kernel(zb_ref, desta_ref, destb_ref, xa_ref, xb_ref, xs_ref, zbuf, sem, zsem, *, nta, tm, ne):
    i = pl.program_id(0)

    @pl.when(i == 0)
    def _():
        zbuf[...] = jnp.zeros_like(zbuf)

        def zero_copy(e):
            return pltpu.make_async_copy(zbuf, xs_ref.at[pl.ds(zb_ref[e] * tm, tm), :], zsem)

        def zstart(e, c):
            @pl.when(zb_ref[e] >= 0)
            def _():
                zero_copy(e).start()
            return c

        def zwait(e, c):
            @pl.when(zb_ref[e] >= 0)
            def _():
                zero_copy(e).wait()
            return c

        lax.fori_loop(0, ne, zstart, 0)
        lax.fori_loop(0, ne, zwait, 0)

    def scatter(x_ref, dest_ref):
        def start(t, c):
            for k in range(TOP_K):
                pltpu.make_async_copy(x_ref.at[pl.ds(t, 1), :], xs_ref.at[pl.ds(dest_ref[0, k, t], 1), :],
                                      sem).start(priority=k % 2)
            return c

        def wait(t, c):
            for k in range(TOP_K):
                pltpu.make_async_copy(x_ref.at[pl.ds(0, 1), :], xs_ref.at[pl.ds(0, 1), :], sem).wait()
            return c

        lax.fori_loop(0, x_ref.shape[0], start, 0)
        lax.fori_loop(0, x_ref.shape[0], wait, 0)

    @pl.when(i < nta)
    def _():
        scatter(xa_ref, desta_ref)

    @pl.when(i == nta)
    def _():
        scatter(xb_ref, destb_ref)


def _dispatch(zero_blk, dest_a, packed_a, dest_b, packed_b, r_max, tm):
    nta, _, tda = dest_a.shape
    _, _, tdb = dest_b.shape
    width = packed_a.shape[1]
    ne = zero_blk.shape[0]
    assert packed_a.shape[0] == nta * tda and packed_b.shape[0] == tdb and dest_b.shape[0] == 1
    return pl.pallas_call(
        functools.partial(_dispatch_kernel, nta=nta, tm=tm, ne=ne),
        grid_spec=pltpu.PrefetchScalarGridSpec(
            num_scalar_prefetch=1,
            grid=(nta + 1,),
            in_specs=[pl.BlockSpec((1, 8, tda), lambda i, zb: (jnp.minimum(i, nta - 1), 0, 0),
                                   memory_space=pltpu.SMEM),
                      pl.BlockSpec((1, 8, tdb), lambda i, zb: (0, 0, 0), memory_space=pltpu.SMEM),
                      pl.BlockSpec((tda, width), lambda i, zb: (jnp.minimum(i, nta - 1), 0)),
                      pl.BlockSpec((tdb, width), lambda i, zb: (0, 0))],
            out_specs=pl.BlockSpec(memory_space=pl.ANY),
            scratch_shapes=[pltpu.VMEM((tm, width), U32), pltpu.SemaphoreType.DMA, pltpu.SemaphoreType.DMA]),
        out_shape=jax.ShapeDtypeStruct((r_max, width), U32),
        compiler_params=_cparams(1),
        name="dispatch",
    )(zero_blk, dest_a, dest_b, packed_a, packed_b)


EXPERT_F_CHUNK = 256
EXPERT_RING = 3


def _expert_kernel(be_ref, nu_ref, xb_ref, x_ref, wg_hbm, wu_hbm, wd_hbm, o_ref, xbf, gu_ring, wd_ring, sem,
                   *, layer, widths):
    del xb_ref
    b = pl.program_id(0)
    n_used = nu_ref[0]
    nc = len(widths)
    starts = [sum(widths[:c]) for c in range(nc)]

    def copies(blk, c):
        e = be_ref[blk]
        slot, c0, w = c % EXPERT_RING, starts[c], widths[c]
        return (
            pltpu.make_async_copy(wg_hbm.at[layer, e, :, pl.ds(c0, w)], gu_ring.at[slot, :, pl.ds(0, w)], sem.at[slot]),
            pltpu.make_async_copy(wu_hbm.at[layer, e, :, pl.ds(c0, w)], gu_ring.at[slot, :, pl.ds(w, w)], sem.at[slot]),
            pltpu.make_async_copy(wd_hbm.at[layer, e, pl.ds(c0, w), :], wd_ring.at[slot, pl.ds(0, w), :], sem.at[slot]))

    def start(blk, c):
        for cp in copies(blk, c):
            cp.start()

    @pl.when(b < n_used)
    def _():
        @pl.when(b == 0)
        def _():
            start(0, 0)
            start(0, 1)

        lo, hi = _unpack_pair(x_ref[...])
        xbf[...] = jnp.concatenate([lo, hi], axis=1).astype(BF16)
        for c in range(nc):
            nxt, ahead = (c + 2) % nc, (c + 2) // nc
            if ahead == 0:
                start(b, nxt)
            else:
                @pl.when(b + ahead < n_used)
                def _():
                    start(b + ahead, nxt)
            for cp in copies(b, c):
                cp.wait()
            slot, w = c % EXPERT_RING, widths[c]
            gu = jnp.dot(xbf[...], gu_ring[slot, :, 0:2 * w].astype(BF16), preferred_element_type=F32)
            h = (_silu(gu[:, :w]) * gu[:, w:]).astype(BF16)
            y = jnp.dot(h, wd_ring[slot, 0:w, :].astype(BF16), preferred_element_type=F32)
            if c == 0:
                o_ref[...] = y
            else:
                o_ref[...] += y


def _experts(layer, block_e, n_used, xblk, xs, w_gate, w_up, w_down, *, tm, n_blocks):
    _, ne, d, f = w_gate.shape
    width = xs.shape[1]
    tf = min(EXPERT_F_CHUNK, f)
    widths = (tf,) * (f // tf) + ((f % tf,) if f % tf else ())
    assert len(widths) >= EXPERT_RING and len(widths) % EXPERT_RING == 0, widths
    return pl.pallas_call(
        functools.partial(_expert_kernel, layer=layer, widths=widths),
        grid_spec=pltpu.PrefetchScalarGridSpec(
            num_scalar_prefetch=3,
            grid=(n_blocks,),
            in_specs=[pl.BlockSpec((tm, width), lambda b, be, nu, xb: (xb[b], 0)),
                      pl.BlockSpec(memory_space=pl.ANY), pl.BlockSpec(memory_space=pl.ANY),
                      pl.BlockSpec(memory_space=pl.ANY)],
            out_specs=pl.BlockSpec((tm, d), lambda b, be, nu, xb: (xb[b], 0)),
            scratch_shapes=[pltpu.VMEM((tm, d), BF16),
                            pltpu.VMEM((EXPERT_RING, d, 2 * tf), F32),
                            pltpu.VMEM((EXPERT_RING, tf, d), F32),
                            pltpu.SemaphoreType.DMA((EXPERT_RING,))]),
        out_shape=jax.ShapeDtypeStruct((n_blocks * tm, d), F32),
        compiler_params=_cparams(1),
        name="experts",
    )(block_e, n_used, xblk, xs, w_gate, w_up, w_down)


def _combine_kernel(dest_ref, x_ref, gt_ref, w_ref, ysh_ref, yb_ref, o_ref, gbuf, sem, *, tmc, per_tile):
    off = (pl.program_id(1) % per_tile) * tmc

    def start(t, c):
        for k in range(TOP_K):
            pltpu.make_async_copy(yb_ref.at[pl.ds(dest_ref[0, k, off + t], 1), :], gbuf.at[k, pl.ds(t, 1), :],
                                  sem).start(priority=k % 2)
        return c

    def wait(t, c):
        for k in range(TOP_K):
            pltpu.make_async_copy(yb_ref.at[pl.ds(0, 1), :], gbuf.at[0, pl.ds(0, 1), :], sem).wait()
        return c

    lax.fori_loop(0, tmc, start, 0)
    lax.fori_loop(0, tmc, wait, 0)
    w = w_ref[...]
    acc = w[:, 0:1] * gbuf[0]
    for k in range(1, TOP_K):
        acc = acc + w[:, k:k + 1] * gbuf[k]
    o_ref[0] = x_ref[0] + gt_ref[0] * (acc + ysh_ref[...])


def _combine(x, gate, dest, w_tok, ysh, yb):
    bx, l, d = x.shape
    tr = dest.shape[2]
    tmc = _pick(tr, (256, 128, 64, 32, 16))
    per_tile = tr // tmc
    nl = l // tmc
    assert l % tr == 0
    gspec = (pl.BlockSpec((1, 1, d), lambda b, i: (b, 0, 0)) if gate.shape[1] == 1
             else pl.BlockSpec((1, tmc, d), lambda b, i: (b, i, 0)))
    return pl.pallas_call(
        functools.partial(_combine_kernel, tmc=tmc, per_tile=per_tile),
        grid=(bx, nl),
        in_specs=[pl.BlockSpec((1, 8, tr), lambda b, i: ((b * nl + i) // per_tile, 0, 0), memory_space=pltpu.SMEM),
                  pl.BlockSpec((1, tmc, d), lambda b, i: (b, i, 0)),
                  gspec,
                  pl.BlockSpec((tmc, 8), lambda b, i: (b * nl + i, 0)),
                  pl.BlockSpec((tmc, d), lambda b, i: (b * nl + i, 0)),
                  pl.BlockSpec(memory_space=pl.ANY)],
        out_specs=pl.BlockSpec((1, tmc, d), lambda b, i: (b, i, 0)),
        out_shape=jax.ShapeDtypeStruct((bx, l, d), F32),
        scratch_shapes=[pltpu.VMEM((TOP_K, tmc, d), F32), pltpu.SemaphoreType.DMA],
        compiler_params=_cparams(2),
        name="combine",
    )(dest, x, gate, w_tok, ysh, yb)


def _expert_block_rows(t_all, ne):
    avg = max(1, t_all * TOP_K // ne)
    return 896 if avg >= 512 else max(32, 2 << avg.bit_length())


def _moe(layer, xa, xb, mods_a, mods_b, norm_g, router_w, router_bias, w_gate, w_up, w_down, ws_gate, ws_up, ws_down):
    _, ne, d, _ = w_gate.shape
    ta, tb = xa.shape[0] * xa.shape[1], xb.shape[1]
    t_all = ta + tb
    tm = _expert_block_rows(t_all, ne)
    n_blocks = (t_all * TOP_K + ne * (tm - 1)) // tm

    router_wt = router_w.T
    packed_a, lg_a = _norm_ffn(xa, norm_g, mods_a[0], mods_a[1], router_wt)
    packed_b, lg_b = _norm_ffn(xb, norm_g, mods_b[0], mods_b[1], router_wt)
    tra = _pick(xa.shape[1], (512, 256, 128, 64, 32, 16))
    eid_a, rank_a, w_a, cnt_a = _router(lg_a, router_bias, jnp.zeros((ne, LANE), F32), tra)
    eid_b, rank_b, w_b, cnt = _router(lg_b, router_bias, cnt_a, tb)

    counts = cnt[:, 0].astype(I32)
    nblk = (counts + tm - 1) // tm
    bend = jnp.cumsum(nblk)
    bstart = bend - nblk
    n_used = bend[-1]
    bidx = jnp.arange(n_blocks, dtype=I32)
    last = jnp.maximum(n_used - 1, 0)
    cb = jnp.minimum(bidx, last)
    block_e = jnp.minimum(jnp.sum((bend[None, :] <= cb[:, None]).astype(I32), axis=1), ne - 1)
    zero_blk = jnp.where(counts > 0, bend - 1, -1).astype(I32)
    pstart = (bstart * tm).astype(I32)
    eids = jnp.arange(ne, dtype=I32)

    def sorted_rows(eid, rank):
        return rank + jnp.sum(jnp.where(eid[..., None] == eids, pstart, 0), axis=-1)

    dest_a, dest_b = sorted_rows(eid_a, rank_a), sorted_rows(eid_b, rank_b)
    xs_sorted = _dispatch(zero_blk, dest_a, packed_a, dest_b, packed_b, n_blocks * tm, tm)
    yb = _experts(layer, block_e, n_used.reshape(1), cb, xs_sorted, w_gate, w_up, w_down, tm=tm, n_blocks=n_blocks)

    outs = []
    for x, packed, mods, dest, w in ((xa, packed_a, mods_a, dest_a, w_a), (xb, packed_b, mods_b, dest_b, w_b)):
        nt = packed.shape[0]
        tms = _pick(nt, (1024, 512, 256, 128, 64, 32, 16))
        nsb = nt // tms
        ysh = _experts(layer, jnp.zeros((nsb,), I32), jnp.full((1,), nsb, I32), jnp.arange(nsb, dtype=I32), packed,
                       ws_gate[:, None], ws_up[:, None], ws_down[:, None], tm=tms, n_blocks=nsb)
        outs.append(_combine(x, mods[2], dest, w.T, ysh, yb))
    return outs


def kernel(x_prompt, x_sample, state_lru_h, state_lru_conv, state_pool, c_prompt, c_sample, ada_w, ada_b, norm_mix, norm_ffn, norm_out, lru_w_in, lru_conv_w, lru_conv_b, lru_w_rg, lru_b_rg, lru_w_ig, lru_b_ig, lru_lambda, lru_w_out, pool_w, pool_scale, router_w, router_bias, exp_w_gate, exp_w_up, exp_w_down, shared_w_gate, shared_w_up, shared_w_down):
    bp, seq, d = x_prompt.shape
    bs = x_sample.shape[0]
    depth = ada_w.shape[0]
    ch = lru_w_in.shape[2] // 2
    assert x_sample.shape[1] == 1 and seq >= 16

    bc = -(-(bp + bs) // 8) * 8
    c_all = jnp.concatenate([c_prompt, c_sample, jnp.zeros((bc - bp - bs, d), F32)], axis=0)
    mod = _ada(c_all, ada_w, ada_b)

    def mods(i):
        mp = [mod[i, :bp, k * d:(k + 1) * d].reshape(bp, 1, d) for k in range(6)]
        ms = [mod[i, bp:bp + bs, k * d:(k + 1) * d].reshape(1, bs, d) for k in range(6)]
        return mp, ms

    xp = x_prompt
    xs = x_sample.reshape(1, bs, d)
    tn_in = _pick(ch, (896, 640, 512, 384, 256, 128))
    tn_out = _pick(d, (512, 256, 128))
    new_h_p, new_cv_p, new_pb_p, new_h_s, new_cv_s, new_pb_s = [], [], [], [], [], []
    for i in range(depth):
        mp, ms = mods(i)
        j = i // 2
        if i % 2 == 0:
            w_in = lru_w_in[j]
            lru = (lru_conv_w[j], lru_conv_b[j], lru_w_rg[j], lru_w_ig[j], lru_b_rg[j], lru_b_ig[j], lru_lambda[j])
            u = _norm_mix(xp, norm_mix[i], mp[0], mp[1])
            gate = _mm(u, w_in, col0=0, ncols=ch, tn=tn_in, act="gelu", name="lru_in_gate")
            rec = _mm(u, w_in, col0=ch, ncols=ch, tn=tn_in, name="lru_in_rec")
            ypre, h_last = _lru_seq(rec, gate, jnp.zeros((bp, ch), F32), jnp.zeros((bp, 3, ch), F32), *lru)
            xp = _mm(ypre, lru_w_out[j], tn=tn_out, resid=xp, gate=mp[2], name="lru_out")
            new_h_p.append(h_last.reshape(bp, ch))
            new_cv_p.append(rec[:, seq - 3:, :])
            u = _norm_mix(xs, norm_mix[i], ms[0], ms[1])
            gate = _mm(u, w_in, col0=0, ncols=ch, tn=tn_in, act="gelu", name="lru_in_gate")
            rec = _mm(u, w_in, col0=ch, ncols=ch, tn=tn_in, name="lru_in_rec")
            cv0 = state_lru_conv[j]
            ypre, h_new = _lru_step(rec[0], gate[0], state_lru_h[j], cv0.transpose(1, 0, 2), *lru)
            xs = _mm(ypre[None], lru_w_out[j], tn=tn_out, resid=xs, gate=ms[2], name="lru_out")
            new_h_s.append(h_new)
            new_cv_s.append(jnp.concatenate([cv0[:, 1:, :], rec[0][:, None, :]], axis=1))
        else:
            xp, pb = _pool_seq(xp, norm_mix[i], mp[0], mp[1], mp[2], jnp.zeros((bp, POOL_BUF, d), F32),
                               pool_w[j], pool_scale[j], 0)
            new_pb_p.append(pb)
            buf0 = state_pool[j]
            xs2, u_s = _pool_step(xs[0], norm_mix[i], ms[0][0], ms[1][0], ms[2][0], buf0.transpose(1, 0, 2),
                                  pool_w[j], pool_scale[j], PAST_LEN)
            xs = xs2[None]
            new_pb_s.append(jnp.concatenate([buf0[:, 1:, :], u_s[:, None, :]], axis=1))
        xp, xs = _moe(i, xp, xs, (mp[3], mp[4], mp[5]), (ms[3], ms[4], ms[5]), norm_ffn[i],
                      router_w[i], router_bias[i], exp_w_gate, exp_w_up, exp_w_down,
                      shared_w_gate, shared_w_up, shared_w_down)
    y_prompt = _norm_out(xp, norm_out)
    y_sample = _norm_out(xs, norm_out).reshape(bs, 1, d)
    return (y_prompt, y_sample, jnp.stack(new_h_p), jnp.stack(new_cv_p), jnp.stack(new_pb_p),
            jnp.stack(new_h_s), jnp.stack(new_cv_s), jnp.stack(new_pb_s))
```

```python
import functools

import jax
import jax.numpy as jnp
from jax import lax
from jax.experimental import pallas as pl
from jax.experimental.pallas import tpu as pltpu

F32 = jnp.float32
BF16 = jnp.bfloat16
I32 = jnp.int32
U32 = jnp.uint32

EPS = 1e-6
LRU_C = 8.0
LRU_BLOCK = 128
POOL_WINDOWS = (2, 4, 8, 16)
POOL_BUF = max(POOL_WINDOWS) - 1
TOP_K = 6
N_GROUPS = 8
TOPK_GROUPS = 4
ROUTED_SCALE = 2.5
PAST_LEN = 16384

V7X_VMEM_LIMIT_BYTES = 58 * 1024 * 1024
LANE = 128


def _cparams(n_axes):
    return pltpu.CompilerParams(dimension_semantics=("arbitrary",) * n_axes,
                                vmem_limit_bytes=V7X_VMEM_LIMIT_BYTES)


def _pick(n, candidates):
    for c in candidates:
        if c <= n and n % c == 0:
            return c
    return n


def _bdot(a, b):
    return jnp.dot(a.astype(BF16), b.astype(BF16), preferred_element_type=F32)


def _silu(x):
    return x * jax.nn.sigmoid(x)


def _ada_kernel(c_ref, w_ref, b_ref, o_ref):
    s = _silu(c_ref[...])
    o_ref[0] = _bdot(s, w_ref[0]) + b_ref[0]


def _ada(c_all, ada_w, ada_b):
    depth, d, n = ada_w.shape
    bc = c_all.shape[0]
    tn = _pick(n, (1024, 512, 256, 128))
    return pl.pallas_call(
        _ada_kernel,
        grid=(depth, n // tn),
        in_specs=[pl.BlockSpec((bc, d), lambda l, j: (0, 0)),
                  pl.BlockSpec((1, d, tn), lambda l, j: (l, 0, j)),
                  pl.BlockSpec((1, 1, tn), lambda l, j: (l, 0, j))],
        out_specs=pl.BlockSpec((1, bc, tn), lambda l, j: (l, 0, j)),
        out_shape=jax.ShapeDtypeStruct((depth, bc, n), F32),
        compiler_params=_cparams(2),
        name="ada_mod",
    )(c_all, ada_w, ada_b.reshape(depth, 1, n))


def _rms(x, g):
    return x * lax.rsqrt(jnp.mean(x * x, axis=-1, keepdims=True) + EPS) * g


def _modnorm(x, g, shift, scale):
    return _rms(x, g) * (1.0 + scale) + shift


def _mod_spec(mod, tm):
    d = mod.shape[-1]
    if mod.shape[1] == 1:
        return pl.BlockSpec((1, 1, d), lambda b, i: (b, 0, 0))
    return pl.BlockSpec((1, tm, d), lambda b, i: (b, i, 0))


def _norm_mix_kernel(x_ref, g_ref, sh_ref, sc_ref, o_ref):
    o_ref[0] = _modnorm(x_ref[0], g_ref[...], sh_ref[0], sc_ref[0]).astype(o_ref.dtype)


def _norm_mix(x, g, shift, scale):
    bx, l, d = x.shape
    tm = _pick(l, (512, 256, 128, 64, 32, 16))
    return pl.pallas_call(
        _norm_mix_kernel,
        grid=(bx, l // tm),
        in_specs=[pl.BlockSpec((1, tm, d), lambda b, i: (b, i, 0)),
                  pl.BlockSpec((1, d), lambda b, i: (0, 0)),
                  _mod_spec(shift, tm), _mod_spec(scale, tm)],
        out_specs=pl.BlockSpec((1, tm, d), lambda b, i: (b, i, 0)),
        out_shape=jax.ShapeDtypeStruct((bx, l, d), BF16),
        compiler_params=_cparams(2),
        name="norm_mix",
    )(x, g.reshape(1, d), shift, scale)


def _norm_out_kernel(x_ref, g_ref, o_ref):
    o_ref[0] = _rms(x_ref[0], g_ref[...])


def _norm_out(x, g):
    bx, l, d = x.shape
    tm = _pick(l, (512, 256, 128, 64, 32, 16, 8))
    return pl.pallas_call(
        _norm_out_kernel,
        grid=(bx, l // tm),
        in_specs=[pl.BlockSpec((1, tm, d), lambda b, i: (b, i, 0)),
                  pl.BlockSpec((1, d), lambda b, i: (0, 0))],
        out_specs=pl.BlockSpec((1, tm, d), lambda b, i: (b, i, 0)),
        out_shape=jax.ShapeDtypeStruct((bx, l, d), F32),
        compiler_params=_cparams(2),
        name="norm_out",
    )(x, g.reshape(1, d))


def _pack_pair(lo, hi):
    lo_b = lax.bitcast_convert_type(lo.astype(BF16).astype(F32), U32)
    hi_b = lax.bitcast_convert_type(hi.astype(BF16).astype(F32), U32)
    return lax.shift_right_logical(lo_b, jnp.uint32(16)) | (hi_b & jnp.uint32(0xFFFF0000))


def _unpack_pair(word):
    lo = lax.bitcast_convert_type(lax.shift_left(word, jnp.uint32(16)), F32)
    hi = lax.bitcast_convert_type(word & jnp.uint32(0xFFFF0000), F32)
    return lo, hi


def _norm_ffn_kernel(x_ref, g_ref, sh_ref, sc_ref, rwt_ref, up_ref, lg_ref):
    u = _modnorm(x_ref[0], g_ref[...], sh_ref[0], sc_ref[0])
    half = u.shape[1] // 2
    up_ref[...] = _pack_pair(u[:, :half], u[:, half:])
    lg_ref[...] = lax.dot_general(rwt_ref[...], u, (((1,), (1,)), ((), ())),
                                  precision=lax.Precision.HIGHEST, preferred_element_type=F32)


def _norm_ffn(x, g, shift, scale, router_wt):
    bx, l, d = x.shape
    e = router_wt.shape[0]
    tm = _pick(l, (512, 256, 128))
    nl = l // tm
    return pl.pallas_call(
        _norm_ffn_kernel,
        grid=(bx, nl),
        in_specs=[pl.BlockSpec((1, tm, d), lambda b, i: (b, i, 0)),
                  pl.BlockSpec((1, d), lambda b, i: (0, 0)),
                  _mod_spec(shift, tm), _mod_spec(scale, tm),
                  pl.BlockSpec((e, d), lambda b, i: (0, 0))],
        out_specs=[pl.BlockSpec((tm, d // 2), lambda b, i: (b * nl + i, 0)),
                   pl.BlockSpec((e, tm), lambda b, i: (0, b * nl + i))],
        out_shape=[jax.ShapeDtypeStruct((bx * l, d // 2), U32),
                   jax.ShapeDtypeStruct((e, bx * l), F32)],
        compiler_params=_cparams(2),
        name="norm_ffn",
    )(x, g.reshape(1, d), shift, scale, router_wt)


def _mm_kernel(*refs, act, resid):
    if resid:
        x_ref, w_ref, r_ref, g_ref, o_ref, wbf = refs
    else:
        x_ref, w_ref, o_ref, wbf = refs

    @pl.when((pl.program_id(1) == 0) & (pl.program_id(2) == 0))
    def _():
        wbf[...] = w_ref[...].astype(BF16)

    acc = jnp.dot(x_ref[0], wbf[...], preferred_element_type=F32)
    if act == "gelu":
        acc = jax.nn.gelu(acc)
    if resid:
        acc = r_ref[0] + g_ref[0] * acc
    o_ref[0] = acc.astype(o_ref.dtype)


def _mm(x, w, *, col0=0, ncols=None, tn, act=None, resid=None, gate=None, out_dtype=F32, name="mm"):
    bx, l, k = x.shape
    ncols = w.shape[1] - col0 if ncols is None else ncols
    tm = _pick(l, (512, 256, 128, 64, 32, 16))
    nj = ncols // tn
    j0 = col0 // tn
    in_specs = [pl.BlockSpec((1, tm, k), lambda j, b, i: (b, i, 0)),
                pl.BlockSpec((k, tn), lambda j, b, i: (0, j0 + j))]
    args = [x, w]
    if resid is not None:
        in_specs.append(pl.BlockSpec((1, tm, tn), lambda j, b, i: (b, i, j)))
        if gate.shape[1] == 1:
            in_specs.append(pl.BlockSpec((1, 1, tn), lambda j, b, i: (b, 0, j)))
        else:
            in_specs.append(pl.BlockSpec((1, tm, tn), lambda j, b, i: (b, i, j)))
        args += [resid, gate]
    return pl.pallas_call(
        functools.partial(_mm_kernel, act=act, resid=resid is not None),
        grid=(nj, bx, l // tm),
        in_specs=in_specs,
        out_specs=pl.BlockSpec((1, tm, tn), lambda j, b, i: (b, i, j)),
        out_shape=jax.ShapeDtypeStruct((bx, l, ncols), out_dtype),
        scratch_shapes=[pltpu.VMEM((k, tn), BF16)],
        compiler_params=_cparams(3),
        name=name,
    )(*args)


def _lru_coeffs(xc, wrg, wig, brg, big, lam):
    heads = xc.shape[1] // LRU_BLOCK
    rs, igs = [], []
    for h in range(heads):
        xh = xc[:, h * LRU_BLOCK:(h + 1) * LRU_BLOCK].astype(BF16)
        rs.append(jnp.dot(xh, wrg[h], preferred_element_type=F32))
        igs.append(jnp.dot(xh, wig[h], preferred_element_type=F32))
    r = jax.nn.sigmoid(jnp.concatenate(rs, axis=1) + brg)
    ig = jax.nn.sigmoid(jnp.concatenate(igs, axis=1) + big)
    log_a = -LRU_C * r * jax.nn.softplus(-lam)
    a = jnp.exp(log_a)
    b = jnp.sqrt(1.0 - jnp.exp(2.0 * log_a)) * (ig * xc)
    return a, b


def _lru_seq_kernel(rec_ref, gate_ref, h0_ref, cv0_ref, cw_ref, cb_ref, wrg_ref, wig_ref, brg_ref,
                    big_ref, lam_ref, y_ref, hout_ref, ebuf, hcar, wrgb, wigb, *, tc):
    c = pl.program_id(2)

    @pl.when(c == 0)
    def _():
        ebuf[5:8, :] = cv0_ref[0]
        hcar[...] = h0_ref[0]
        wrgb[...] = wrg_ref[...].astype(BF16)
        wigb[...] = wig_ref[...].astype(BF16)

    rec = rec_ref[0]
    ebuf[8:8 + tc, :] = rec
    cw = cw_ref[...]
    xc = cb_ref[...] + ebuf[5:5 + tc, :] * cw[0:1] + ebuf[6:6 + tc, :] * cw[1:2] \
        + ebuf[7:7 + tc, :] * cw[2:3] + rec * cw[3:4]
    a, b = _lru_coeffs(xc, wrgb, wigb, brg_ref[...], big_ref[...], lam_ref[...])

    row = lax.broadcasted_iota(I32, a.shape, 0)
    s = 1
    while s < tc:
        m = row >= s
        a_sh = pltpu.roll(a, s, 0)
        b_sh = pltpu.roll(b, s, 0)
        b = jnp.where(m, a * b_sh + b, b)
        a = jnp.where(m, a * a_sh, a)
        s *= 2
    h = a * hcar[...] + b
    y_ref[0] = (gate_ref[0] * h).astype(y_ref.dtype)
    hlast = h[tc - 1:tc, :]
    hcar[...] = hlast
    hout_ref[0] = hlast
    ebuf[5:8, :] = rec[tc - 3:tc, :]


def _lru_seq(rec, gate, h0, cv0, conv_w, conv_b, w_rg, w_ig, b_rg, b_ig, lam):
    bx, l, ch = rec.shape
    tc = _pick(l, (256, 128, 64, 32, 16, 8))
    heads = ch // LRU_BLOCK
    hpt = 7 if heads % 7 == 0 else (5 if heads % 5 == 0 else 1)
    ct = hpt * LRU_BLOCK
    nct = ch // ct
    row = lambda a: a.reshape(1, ch)
    cspec = lambda r: pl.BlockSpec((r, ct), lambda b, j, c: (0, j))
    return pl.pallas_call(
        functools.partial(_lru_seq_kernel, tc=tc),
        grid=(bx, nct, l // tc),
        in_specs=[pl.BlockSpec((1, tc, ct), lambda b, j, c: (b, c, j)),
                  pl.BlockSpec((1, tc, ct), lambda b, j, c: (b, c, j)),
                  pl.BlockSpec((1, 1, ct), lambda b, j, c: (b, 0, j)),
                  pl.BlockSpec((1, 3, ct), lambda b, j, c: (b, 0, j)),
                  cspec(4), cspec(1),
                  pl.BlockSpec((hpt, LRU_BLOCK, LRU_BLOCK), lambda b, j, c: (j, 0, 0)),
                  pl.BlockSpec((hpt, LRU_BLOCK, LRU_BLOCK), lambda b, j, c: (j, 0, 0)),
                  cspec(1), cspec(1), cspec(1)],
        out_specs=[pl.BlockSpec((1, tc, ct), lambda b, j, c: (b, c, j)),
                   pl.BlockSpec((1, 1, ct), lambda b, j, c: (b, 0, j))],
        out_shape=[jax.ShapeDtypeStruct((bx, l, ch), BF16),
                   jax.ShapeDtypeStruct((bx, 1, ch), F32)],
        scratch_shapes=[pltpu.VMEM((8 + tc, ct), F32), pltpu.VMEM((1, ct), F32),
                        pltpu.VMEM((hpt, LRU_BLOCK, LRU_BLOCK), BF16),
                        pltpu.VMEM((hpt, LRU_BLOCK, LRU_BLOCK), BF16)],
        compiler_params=_cparams(3),
        name="lru_seq",
    )(rec, gate, h0.reshape(bx, 1, ch), cv0, conv_w, row(conv_b), w_rg, w_ig, row(b_rg), row(b_ig), row(lam))


def _lru_step_kernel(rec_ref, gate_ref, h0_ref, cv0_ref, cw_ref, cb_ref, wrg_ref, wig_ref, brg_ref,
                     big_ref, lam_ref, y_ref, hout_ref):
    rec = rec_ref[...]
    cw = cw_ref[...]
    xc = cb_ref[...] + cv0_ref[0] * cw[0:1] + cv0_ref[1] * cw[1:2] + cv0_ref[2] * cw[2:3] + rec * cw[3:4]
    a, b = _lru_coeffs(xc, wrg_ref[...].astype(BF16), wig_ref[...].astype(BF16),
                       brg_ref[...], big_ref[...], lam_ref[...])
    h = a * h0_ref[...] + b
    y_ref[...] = (gate_ref[...] * h).astype(y_ref.dtype)
    hout_ref[...] = h


def _lru_step(rec, gate, h0, cv0_t, conv_w, conv_b, w_rg, w_ig, b_rg, b_ig, lam):
    bx, ch = rec.shape
    heads = ch // LRU_BLOCK
    hpt = 7 if heads % 7 == 0 else (5 if heads % 5 == 0 else 1)
    ct = hpt * LRU_BLOCK
    row = lambda a: a.reshape(1, ch)
    bspec = pl.BlockSpec((bx, ct), lambda j: (0, j))
    cspec = lambda r: pl.BlockSpec((r, ct), lambda j: (0, j))
    wspec = pl.BlockSpec((hpt, LRU_BLOCK, LRU_BLOCK), lambda j: (j, 0, 0))
    return pl.pallas_call(
        _lru_step_kernel,
        grid=(ch // ct,),
        in_specs=[bspec, bspec, bspec, pl.BlockSpec((3, bx, ct), lambda j: (0, 0, j)),
                  cspec(4), cspec(1), wspec, wspec, cspec(1), cspec(1), cspec(1)],
        out_specs=[bspec, bspec],
        out_shape=[jax.ShapeDtypeStruct((bx, ch), BF16), jax.ShapeDtypeStruct((bx, ch), F32)],
        compiler_params=_cparams(1),
        name="lru_step",
    )(rec, gate, h0, cv0_t, conv_w, row(conv_b), w_rg, w_ig, row(b_rg), row(b_ig), row(lam))


def _pool_seq_kernel(x_ref, g_ref, sh_ref, sc_ref, gt_ref, buf0_ref, pw_ref, ps_ref, o_ref, st_ref,
                     ebuf, pwb, *, tc, pos0):
    c = pl.program_id(1)
    d = x_ref.shape[2]
    gd = d // len(POOL_WINDOWS)

    @pl.when(c == 0)
    def _():
        ebuf[1:16, :] = buf0_ref[0]
        pwb[...] = pw_ref[...].astype(BF16)

    x = x_ref[0]
    u = _modnorm(x, g_ref[...], sh_ref[0], sc_ref[0])
    ebuf[16:16 + tc, :] = u
    pos = pos0 + c * tc + lax.broadcasted_iota(I32, (tc, gd), 0)
    ys = []
    for gi, w in enumerate(POOL_WINDOWS):
        sl = slice(gi * gd, (gi + 1) * gd)
        ssum = u[:, sl]
        for j in range(1, w):
            ssum = ssum + ebuf[16 - j:16 - j + tc, sl]
        cnt = jnp.minimum(w, pos + 1).astype(F32)
        p = ssum / cnt - u[:, sl]
        ys.append(jnp.dot(p.astype(BF16), pwb[gi], preferred_element_type=F32))
    y = jnp.concatenate(ys, axis=1) * ps_ref[...]
    o_ref[0] = x + gt_ref[0] * y
    st_ref[0] = ebuf[tc + 1:tc + 16, :]
    ebuf[0:16, :] = ebuf[tc:tc + 16, :]


def _pool_seq(x, g, shift, scale, gate, buf0, pool_w, pool_scale, pos0):
    bx, l, d = x.shape
    tc = _pick(l, (256, 128, 64, 32, 16))
    ng, gd, _ = pool_w.shape
    mspec = lambda m: (pl.BlockSpec((1, 1, d), lambda b, c: (b, 0, 0)) if m.shape[1] == 1
                       else pl.BlockSpec((1, tc, d), lambda b, c: (b, c, 0)))
    return pl.pallas_call(
        functools.partial(_pool_seq_kernel, tc=tc, pos0=pos0),
        grid=(bx, l // tc),
        in_specs=[pl.BlockSpec((1, tc, d), lambda b, c: (b, c, 0)),
                  pl.BlockSpec((1, d), lambda b, c: (0, 0)),
                  mspec(shift), mspec(scale), mspec(gate),
                  pl.BlockSpec((1, POOL_BUF, d), lambda b, c: (b, 0, 0)),
                  pl.BlockSpec((ng, gd, gd), lambda b, c: (0, 0, 0)),
                  pl.BlockSpec((1, d), lambda b, c: (0, 0))],
        out_specs=[pl.BlockSpec((1, tc, d), lambda b, c: (b, c, 0)),
                   pl.BlockSpec((1, POOL_BUF, d), lambda b, c: (b, 0, 0))],
        out_shape=[jax.ShapeDtypeStruct((bx, l, d), F32),
                   jax.ShapeDtypeStruct((bx, POOL_BUF, d), F32)],
        scratch_shapes=[pltpu.VMEM((16 + tc, d), F32), pltpu.VMEM((ng, gd, gd), BF16)],
        compiler_params=_cparams(2),
        name="pool_seq",
    )(x, g.reshape(1, d), shift, scale, gate, buf0, pool_w, pool_scale.reshape(1, d))


def _pool_step_kernel(x_ref, g_ref, sh_ref, sc_ref, gt_ref, buf_ref, pw_ref, ps_ref, o_ref, u_ref, *, pos0):
    d = x_ref.shape[1]
    gd = d // len(POOL_WINDOWS)
    x = x_ref[...]
    u = _modnorm(x, g_ref[...], sh_ref[...], sc_ref[...])
    ys = []
    for gi, w in enumerate(POOL_WINDOWS):
        sl = slice(gi * gd, (gi + 1) * gd)
        ssum = u[:, sl]
        for j in range(1, w):
            ssum = ssum + buf_ref[POOL_BUF - j, :, sl]
        cnt = float(min(w, pos0 + 1))
        p = ssum / cnt - u[:, sl]
        ys.append(_bdot(p, pw_ref[gi]))
    y = jnp.concatenate(ys, axis=1) * ps_ref[...]
    o_ref[...] = x + gt_ref[...] * y
    u_ref[...] = u


def _pool_step(x, g, shift, scale, gate, buf_t, pool_w, pool_scale, pos0):
    bx, d = x.shape
    ng, gd, _ = pool_w.shape
    full = pl.BlockSpec((bx, d), lambda i: (0, 0))
    vec = pl.BlockSpec((1, d), lambda i: (0, 0))
    return pl.pallas_call(
        functools.partial(_pool_step_kernel, pos0=pos0),
        grid=(1,),
        in_specs=[full, vec, full, full, full,
                  pl.BlockSpec((POOL_BUF, bx, d), lambda i: (0, 0, 0)),
                  pl.BlockSpec((ng, gd, gd), lambda i: (0, 0, 0)), vec],
        out_specs=[full, full],
        out_shape=[jax.ShapeDtypeStruct((bx, d), F32), jax.ShapeDtypeStruct((bx, d), F32)],
        compiler_params=_cparams(1),
        name="pool_step",
    )(x, g.reshape(1, d), shift, scale, gate, buf_t, pool_w, pool_scale.reshape(1, d))


def _first_max(v, idx, n):
    m = jnp.max(v, axis=0, keepdims=True)
    first = jnp.min(jnp.where(v == m, idx, n), axis=0, keepdims=True)
    return jnp.where(idx == first, 1.0, 0.0), m, first


def _stack_rows(rows, dtype):
    n = rows[0].shape[1]
    r8 = lax.broadcasted_iota(I32, (8, n), 0)
    out = jnp.zeros((8, n), dtype)
    for k, r in enumerate(rows):
        out = jnp.where(r8 == k, jnp.broadcast_to(r.astype(dtype), (8, n)), out)
    return out


def _router_kernel(lg_ref, bias_ref, cin_ref, eid_ref, rank_ref, w_ref, cnt_ref, carry, *, tr):
    i = pl.program_id(0)

    @pl.when(i == 0)
    def _():
        carry[...] = cin_ref[:, 0:1]

    ne = lg_ref.shape[0]
    gs = ne // N_GROUPS
    s = jax.nn.sigmoid(lg_ref[...])
    biased = s + bias_ref[...]
    neg = -jnp.inf
    idx8 = lax.broadcasted_iota(I32, (gs, tr), 0)
    gidx = lax.broadcasted_iota(I32, (N_GROUPS, tr), 0)
    gsc = jnp.zeros((N_GROUPS, tr), F32)
    for g in range(N_GROUPS):
        blk = biased[g * gs:(g + 1) * gs, :]
        oh, m1, _ = _first_max(blk, idx8, gs)
        m2 = jnp.max(jnp.where(oh > 0, neg, blk), axis=0, keepdims=True)
        gsc = jnp.where(gidx == g, jnp.broadcast_to(m1 + m2, (N_GROUPS, tr)), gsc)
    gsel = jnp.zeros((N_GROUPS, tr), F32)
    for _ in range(TOPK_GROUPS):
        oh, _, _ = _first_max(gsc, gidx, N_GROUPS)
        gsel = gsel + oh
        gsc = jnp.where(oh > 0, neg, gsc)
    emask = jnp.concatenate([jnp.broadcast_to(gsel[g:g + 1, :], (gs, tr)) for g in range(N_GROUPS)], axis=0)
    masked = jnp.where(emask > 0, biased, neg)
    eidx = lax.broadcasted_iota(I32, (ne, tr), 0)
    ohs, ws, ids = [], [], []
    for _ in range(TOP_K):
        oh, _, first = _first_max(masked, eidx, ne)
        ohs.append(oh)
        ids.append(first)
        ws.append(jnp.sum(oh * s, axis=0, keepdims=True))
        masked = jnp.where(oh > 0, neg, masked)
    wsum = ws[0]
    cm = ohs[0]
    for k in range(1, TOP_K):
        wsum = wsum + ws[k]
        cm = cm + ohs[k]
    tri = (lax.broadcasted_iota(I32, (tr, tr), 0) < lax.broadcasted_iota(I32, (tr, tr), 1))
    prefix = jnp.dot(cm.astype(BF16), jnp.where(tri, 1.0, 0.0).astype(BF16),
                     preferred_element_type=F32) + carry[...]
    rk = [jnp.sum(ohs[k] * prefix, axis=0, keepdims=True) for k in range(TOP_K)]
    rank_ref[0] = _stack_rows(rk, F32).astype(I32)
    eid_ref[0] = _stack_rows(ids, I32)
    w_ref[...] = _stack_rows([w / wsum * ROUTED_SCALE for w in ws], F32)
    carry[...] = carry[...] + jnp.sum(cm, axis=1, keepdims=True)
    cnt_ref[...] = jnp.broadcast_to(carry[...], cnt_ref.shape)


def _router(logits_t, bias, counts_in, tr):
    ne, t = logits_t.shape
    nt = t // tr
    tile = pl.BlockSpec((1, 8, tr), lambda i: (i, 0, 0))
    return pl.pallas_call(
        functools.partial(_router_kernel, tr=tr),
        grid=(nt,),
        in_specs=[pl.BlockSpec((ne, tr), lambda i: (0, i)), pl.BlockSpec((ne, 1), lambda i: (0, 0)),
                  pl.BlockSpec((ne, LANE), lambda i: (0, 0))],
        out_specs=[tile, tile, pl.BlockSpec((8, tr), lambda i: (0, i)),
                   pl.BlockSpec((ne, LANE), lambda i: (0, 0))],
        out_shape=[jax.ShapeDtypeStruct((nt, 8, tr), I32), jax.ShapeDtypeStruct((nt, 8, tr), I32),
                   jax.ShapeDtypeStruct((8, t), F32), jax.ShapeDtypeStruct((ne, LANE), F32)],
        scratch_shapes=[pltpu.VMEM((ne, 1), F32)],
        compiler_params=_cparams(1),
        name="router",
    )(logits_t, bias.reshape(ne, 1), counts_in)


def _dispatch_kernel(zb_ref, desta_ref, destb_ref, xa_ref, xb_ref, xs_ref, zbuf, sem, zsem, *, nta, ne):
    i = pl.program_id(0)
    zrows = zbuf.shape[0]

    @pl.when(i == 0)
    def _():
        zbuf[...] = jnp.zeros_like(zbuf)

        def zero_copy(e):
            return pltpu.make_async_copy(zbuf, xs_ref.at[pl.ds(zb_ref[e] * zrows, zrows), :], zsem)

        def zstart(e, c):
            @pl.when(zb_ref[e] >= 0)
            def _():
                zero_copy(e).start()
            return c

        def zwait(e, c):
            @pl.when(zb_ref[e] >= 0)
            def _():
                zero_copy(e).wait()
            return c

        lax.fori_loop(0, ne, zstart, 0)
        lax.fori_loop(0, ne, zwait, 0)

    def scatter(x_ref, dest_ref):
        def start(t, c):
            for k in range(TOP_K):
                pltpu.make_async_copy(x_ref.at[pl.ds(t, 1), :], xs_ref.at[pl.ds(dest_ref[0, k, t], 1), :],
                                      sem).start(priority=k % 2)
            return c

        def wait(t, c):
            for k in range(TOP_K):
                pltpu.make_async_copy(x_ref.at[pl.ds(0, 1), :], xs_ref.at[pl.ds(0, 1), :], sem).wait()
            return c

        lax.fori_loop(0, x_ref.shape[0], start, 0)
        lax.fori_loop(0, x_ref.shape[0], wait, 0)

    @pl.when(i < nta)
    def _():
        scatter(xa_ref, desta_ref)

    @pl.when(i == nta)
    def _():
        scatter(xb_ref, destb_ref)


def _dispatch(zero_blk, dest_a, packed_a, dest_b, packed_b, r_max, zrows):
    nta, _, tda = dest_a.shape
    _, _, tdb = dest_b.shape
    width = packed_a.shape[1]
    ne = zero_blk.shape[0]
    assert packed_a.shape[0] == nta * tda and packed_b.shape[0] == tdb and dest_b.shape[0] == 1
    return pl.pallas_call(
        functools.partial(_dispatch_kernel, nta=nta, ne=ne),
        grid_spec=pltpu.PrefetchScalarGridSpec(
            num_scalar_prefetch=1,
            grid=(nta + 1,),
            in_specs=[pl.BlockSpec((1, 8, tda), lambda i, zb: (jnp.minimum(i, nta - 1), 0, 0),
                                   memory_space=pltpu.SMEM),
                      pl.BlockSpec((1, 8, tdb), lambda i, zb: (0, 0, 0), memory_space=pltpu.SMEM),
                      pl.BlockSpec((tda, width), lambda i, zb: (jnp.minimum(i, nta - 1), 0)),
                      pl.BlockSpec((tdb, width), lambda i, zb: (0, 0))],
            out_specs=pl.BlockSpec(memory_space=pl.ANY),
            scratch_shapes=[pltpu.VMEM((zrows, width), U32), pltpu.SemaphoreType.DMA, pltpu.SemaphoreType.DMA]),
        out_shape=jax.ShapeDtypeStruct((r_max, width), U32),
        compiler_params=_cparams(1),
        name="dispatch",
    )(zero_blk, dest_a, dest_b, packed_a, packed_b)


EXPERT_F_CHUNK = 256
EXPERT_RING = 3


def _expert_kernel(be_ref, nr_ref, xb_ref, x_ref, wg_hbm, wu_hbm, wd_hbm, o_ref, xbf, gu_ring, wd_ring, sem,
                   *, layer, widths, sub, full):
    del xb_ref
    b = pl.program_id(0)
    n_used = nr_ref[pl.num_programs(0)]
    n = nr_ref[b]
    nsub = x_ref.shape[0] // sub
    nc = len(widths)
    starts = [sum(widths[:c]) for c in range(nc)]

    def copies(blk, c):
        e = be_ref[blk]
        slot, c0, w = c % EXPERT_RING, starts[c], widths[c]
        return (
            pltpu.make_async_copy(wg_hbm.at[layer, e, :, pl.ds(c0, w)], gu_ring.at[slot, :, pl.ds(0, w)], sem.at[slot]),
            pltpu.make_async_copy(wu_hbm.at[layer, e, :, pl.ds(c0, w)], gu_ring.at[slot, :, pl.ds(w, w)], sem.at[slot]),
            pltpu.make_async_copy(wd_hbm.at[layer, e, pl.ds(c0, w), :], wd_ring.at[slot, pl.ds(0, w), :], sem.at[slot]))

    def start(blk, c):
        for cp in copies(blk, c):
            cp.start()

    @pl.when(b < n_used)
    def _():
        @pl.when(b == 0)
        def _():
            start(0, 0)
            start(0, 1)

        def unpack(rows):
            lo, hi = _unpack_pair(x_ref[rows, :])
            xbf[rows, :] = jnp.concatenate([lo, hi], axis=1).astype(BF16)

        for s in range(nsub):
            rows = slice(s * sub, (s + 1) * sub)
            if full:
                unpack(rows)
            else:
                pl.when(s * sub < n)(functools.partial(unpack, rows))

        for c in range(nc):
            nxt, ahead = (c + 2) % nc, (c + 2) // nc
            if ahead == 0:
                start(b, nxt)
            else:
                @pl.when(b + ahead < n_used)
                def _():
                    start(b + ahead, nxt)
            for cp in copies(b, c):
                cp.wait()
            slot, w = c % EXPERT_RING, widths[c]

            def part(r0, nr):
                rows = slice(r0, r0 + nr)
                gu = jnp.dot(xbf[rows, :], gu_ring[slot, :, 0:2 * w].astype(BF16), preferred_element_type=F32)
                h = (_silu(gu[:, :w]) * gu[:, w:]).astype(BF16)
                y = jnp.dot(h, wd_ring[slot, 0:w, :].astype(BF16), preferred_element_type=F32)
                if c == 0:
                    o_ref[rows, :] = y
                else:
                    o_ref[rows, :] += y

            if full:
                part(0, nsub * sub)
            else:
                for r0 in (0, 2 * sub):
                    pl.when(n > r0 + sub)(functools.partial(part, r0, 2 * sub))
                    pl.when((n > r0) & (n <= r0 + sub))(functools.partial(part, r0, sub))


def _experts(layer, block_e, nrows, xblk, xs, w_gate, w_up, w_down, *, tm, sub, n_blocks, full=False):
    _, ne, d, f = w_gate.shape
    width = xs.shape[1]
    tf = min(EXPERT_F_CHUNK, f)
    widths = (tf,) * (f // tf) + ((f % tf,) if f % tf else ())
    assert len(widths) >= EXPERT_RING and len(widths) % EXPERT_RING == 0, widths
    assert (full or tm == 4 * sub) and tm % sub == 0 and nrows.shape[0] == n_blocks + 1
    return pl.pallas_call(
        functools.partial(_expert_kernel, layer=layer, widths=widths, sub=sub, full=full),
        grid_spec=pltpu.PrefetchScalarGridSpec(
            num_scalar_prefetch=3,
            grid=(n_blocks,),
            in_specs=[pl.BlockSpec((tm, width), lambda b, be, nu, xb: (xb[b], 0)),
                      pl.BlockSpec(memory_space=pl.ANY), pl.BlockSpec(memory_space=pl.ANY),
                      pl.BlockSpec(memory_space=pl.ANY)],
            out_specs=pl.BlockSpec((tm, d), lambda b, be, nu, xb: (xb[b], 0)),
            scratch_shapes=[pltpu.VMEM((tm, d), BF16),
                            pltpu.VMEM((EXPERT_RING, d, 2 * tf), F32),
                            pltpu.VMEM((EXPERT_RING, tf, d), F32),
                            pltpu.SemaphoreType.DMA((EXPERT_RING,))]),
        out_shape=jax.ShapeDtypeStruct((n_blocks * tm, d), F32),
        compiler_params=_cparams(1),
        name="experts",
    )(block_e, nrows, xblk, xs, w_gate, w_up, w_down)


def _combine_kernel(dest_ref, x_ref, gt_ref, w_ref, ysh_ref, yb_ref, o_ref, gbuf, sem, *, tmc, per_tile):
    off = (pl.program_id(1) % per_tile) * tmc

    def start(t, c):
        for k in range(TOP_K):
            pltpu.make_async_copy(yb_ref.at[pl.ds(dest_ref[0, k, off + t], 1), :], gbuf.at[k, pl.ds(t, 1), :],
                                  sem).start(priority=k % 2)
        return c

    def wait(t, c):
        for k in range(TOP_K):
            pltpu.make_async_copy(yb_ref.at[pl.ds(0, 1), :], gbuf.at[0, pl.ds(0, 1), :], sem).wait()
        return c

    lax.fori_loop(0, tmc, start, 0)
    lax.fori_loop(0, tmc, wait, 0)
    w = w_ref[...]
    acc = w[:, 0:1] * gbuf[0]
    for k in range(1, TOP_K):
        acc = acc + w[:, k:k + 1] * gbuf[k]
    o_ref[0] = x_ref[0] + gt_ref[0] * (acc + ysh_ref[...])


def _combine(x, gate, dest, w_tok, ysh, yb):
    bx, l, d = x.shape
    tr = dest.shape[2]
    tmc = _pick(tr, (256, 128, 64, 32, 16))
    per_tile = tr // tmc
    nl = l // tmc
    assert l % tr == 0
    gspec = (pl.BlockSpec((1, 1, d), lambda b, i: (b, 0, 0)) if gate.shape[1] == 1
             else pl.BlockSpec((1, tmc, d), lambda b, i: (b, i, 0)))
    return pl.pallas_call(
        functools.partial(_combine_kernel, tmc=tmc, per_tile=per_tile),
        grid=(bx, nl),
        in_specs=[pl.BlockSpec((1, 8, tr), lambda b, i: ((b * nl + i) // per_tile, 0, 0), memory_space=pltpu.SMEM),
                  pl.BlockSpec((1, tmc, d), lambda b, i: (b, i, 0)),
                  gspec,
                  pl.BlockSpec((tmc, 8), lambda b, i: (b * nl + i, 0)),
                  pl.BlockSpec((tmc, d), lambda b, i: (b * nl + i, 0)),
                  pl.BlockSpec(memory_space=pl.ANY)],
        out_specs=pl.BlockSpec((1, tmc, d), lambda b, i: (b, i, 0)),
        out_shape=jax.ShapeDtypeStruct((bx, l, d), F32),
        scratch_shapes=[pltpu.VMEM((TOP_K, tmc, d), F32), pltpu.SemaphoreType.DMA],
        compiler_params=_cparams(2),
        name="combine",
    )(dest, x, gate, w_tok, ysh, yb)


def _expert_block_rows(t_all, ne):
    avg = max(1, t_all * TOP_K // ne)
    if avg >= 512:
        return 1024, 256
    tm = max(64, 2 << avg.bit_length())
    return tm, tm // 4


def _moe(layer, xa, xb, mods_a, mods_b, norm_g, router_w, router_bias, w_gate, w_up, w_down, ws_gate, ws_up, ws_down):
    _, ne, d, _ = w_gate.shape
    ta, tb = xa.shape[0] * xa.shape[1], xb.shape[1]
    t_all = ta + tb
    tm, sub = _expert_block_rows(t_all, ne)
    n_blocks = (t_all * TOP_K + ne * (tm - 1)) // tm

    router_wt = router_w.T
    packed_a, lg_a = _norm_ffn(xa, norm_g, mods_a[0], mods_a[1], router_wt)
    packed_b, lg_b = _norm_ffn(xb, norm_g, mods_b[0], mods_b[1], router_wt)
    tra = _pick(xa.shape[1], (512, 256, 128, 64, 32, 16))
    eid_a, rank_a, w_a, cnt_a = _router(lg_a, router_bias, jnp.zeros((ne, LANE), F32), tra)
    eid_b, rank_b, w_b, cnt = _router(lg_b, router_bias, cnt_a, tb)

    counts = cnt[:, 0].astype(I32)
    nblk = (counts + tm - 1) // tm
    bend = jnp.cumsum(nblk)
    bstart = bend - nblk
    n_used = bend[-1]
    bidx = jnp.arange(n_blocks, dtype=I32)
    last = jnp.maximum(n_used - 1, 0)
    cb = jnp.minimum(bidx, last)
    block_e = jnp.minimum(jnp.sum((bend[None, :] <= cb[:, None]).astype(I32), axis=1), ne - 1)
    eids = jnp.arange(ne, dtype=I32)
    of_block = lambda v: jnp.sum(jnp.where(block_e[:, None] == eids, v, 0), axis=1)
    nrows = jnp.where(bidx < n_used, jnp.clip(of_block(counts) - (bidx - of_block(bstart)) * tm, 0, tm), 0)
    nrows = jnp.concatenate([nrows, n_used.reshape(1)]).astype(I32)
    pstart = (bstart * tm).astype(I32)
    zero_blk = jnp.where(counts > 0, pstart // sub + (counts - 1) // sub, -1).astype(I32)

    def sorted_rows(eid, rank):
        return rank + jnp.sum(jnp.where(eid[..., None] == eids, pstart, 0), axis=-1)

    dest_a, dest_b = sorted_rows(eid_a, rank_a), sorted_rows(eid_b, rank_b)
    xs_sorted = _dispatch(zero_blk, dest_a, packed_a, dest_b, packed_b, n_blocks * tm, sub)
    yb = _experts(layer, block_e, nrows, cb, xs_sorted, w_gate, w_up, w_down, tm=tm, sub=sub, n_blocks=n_blocks)

    outs = []
    for x, packed, mods, dest, w in ((xa, packed_a, mods_a, dest_a, w_a), (xb, packed_b, mods_b, dest_b, w_b)):
        nt = packed.shape[0]
        tms = _pick(nt, (1024, 512, 256, 128, 64, 32, 16))
        nsb = nt // tms
        ysh = _experts(layer, jnp.zeros((nsb,), I32), jnp.array([tms] * nsb + [nsb], I32), jnp.arange(nsb, dtype=I32),
                       packed, ws_gate[:, None], ws_up[:, None], ws_down[:, None],
                       tm=tms, sub=tms, n_blocks=nsb, full=True)
        outs.append(_combine(x, mods[2], dest, w.T, ysh, yb))
    return outs


def kernel(x_prompt, x_sample, state_lru_h, state_lru_conv, state_pool, c_prompt, c_sample, ada_w, ada_b, norm_mix, norm_ffn, norm_out, lru_w_in, lru_conv_w, lru_conv_b, lru_w_rg, lru_b_rg, lru_w_ig, lru_b_ig, lru_lambda, lru_w_out, pool_w, pool_scale, router_w, router_bias, exp_w_gate, exp_w_up, exp_w_down, shared_w_gate, shared_w_up, shared_w_down):
    bp, seq, d = x_prompt.shape
    bs = x_sample.shape[0]
    depth = ada_w.shape[0]
    ch = lru_w_in.shape[2] // 2
    assert x_sample.shape[1] == 1 and seq >= 16

    bc = -(-(bp + bs) // 8) * 8
    c_all = jnp.concatenate([c_prompt, c_sample, jnp.zeros((bc - bp - bs, d), F32)], axis=0)
    mod = _ada(c_all, ada_w, ada_b)

    def mods(i):
        mp = [mod[i, :bp, k * d:(k + 1) * d].reshape(bp, 1, d) for k in range(6)]
        ms = [mod[i, bp:bp + bs, k * d:(k + 1) * d].reshape(1, bs, d) for k in range(6)]
        return mp, ms

    xp = x_prompt
    xs = x_sample.reshape(1, bs, d)
    tn_in = _pick(ch, (896, 640, 512, 384, 256, 128))
    tn_out = _pick(d, (512, 256, 128))
    new_h_p, new_cv_p, new_pb_p, new_h_s, new_cv_s, new_pb_s = [], [], [], [], [], []
    for i in range(depth):
        mp, ms = mods(i)
        j = i // 2
        if i % 2 == 0:
            w_in = lru_w_in[j]
            lru = (lru_conv_w[j], lru_conv_b[j], lru_w_rg[j], lru_w_ig[j], lru_b_rg[j], lru_b_ig[j], lru_lambda[j])
            u = _norm_mix(xp, norm_mix[i], mp[0], mp[1])
            gate = _mm(u, w_in, col0=0, ncols=ch, tn=tn_in, act="gelu", name="lru_in_gate")
            rec = _mm(u, w_in, col0=ch, ncols=ch, tn=tn_in, name="lru_in_rec")
            ypre, h_last = _lru_seq(rec, gate, jnp.zeros((bp, ch), F32), jnp.zeros((bp, 3, ch), F32), *lru)
            xp = _mm(ypre, lru_w_out[j], tn=tn_out, resid=xp, gate=mp[2], name="lru_out")
            new_h_p.append(h_last.reshape(bp, ch))
            new_cv_p.append(rec[:, seq - 3:, :])
            u = _norm_mix(xs, norm_mix[i], ms[0], ms[1])
            gate = _mm(u, w_in, col0=0, ncols=ch, tn=tn_in, act="gelu", name="lru_in_gate")
            rec = _mm(u, w_in, col0=ch, ncols=ch, tn=tn_in, name="lru_in_rec")
            cv0 = state_lru_conv[j]
            ypre, h_new = _lru_step(rec[0], gate[0], state_lru_h[j], cv0.transpose(1, 0, 2), *lru)
            xs = _mm(ypre[None], lru_w_out[j], tn=tn_out, resid=xs, gate=ms[2], name="lru_out")
            new_h_s.append(h_new)
            new_cv_s.append(jnp.concatenate([cv0[:, 1:, :], rec[0][:, None, :]], axis=1))
        else:
            xp, pb = _pool_seq(xp, norm_mix[i], mp[0], mp[1], mp[2], jnp.zeros((bp, POOL_BUF, d), F32),
                               pool_w[j], pool_scale[j], 0)
            new_pb_p.append(pb)
            buf0 = state_pool[j]
            xs2, u_s = _pool_step(xs[0], norm_mix[i], ms[0][0], ms[1][0], ms[2][0], buf0.transpose(1, 0, 2),
                                  pool_w[j], pool_scale[j], PAST_LEN)
            xs = xs2[None]
            new_pb_s.append(jnp.concatenate([buf0[:, 1:, :], u_s[:, None, :]], axis=1))
        xp, xs = _moe(i, xp, xs, (mp[3], mp[4], mp[5]), (ms[3], ms[4], ms[5]), norm_ffn[i],
                      router_w[i], router_bias[i], exp_w_gate, exp_w_up, exp_w_down,
                      shared_w_gate, shared_w_up, shared_w_down)
    y_prompt = _norm_out(xp, norm_out)
    y_sample = _norm_out(xs, norm_out).reshape(bs, 1, d)
    return (y_prompt, y_sample, jnp.stack(new_h_p), jnp.stack(new_cv_p), jnp.stack(new_pb_p),
            jnp.stack(new_h_s), jnp.stack(new_cv_s), jnp.stack(new_pb_s))
```

```python
import functools

import jax
import jax.numpy as jnp
from jax import lax
from jax.experimental import pallas as pl
from jax.experimental.pallas import tpu as pltpu

F32 = jnp.float32
BF16 = jnp.bfloat16
I32 = jnp.int32
U32 = jnp.uint32

EPS = 1e-6
LRU_C = 8.0
LRU_BLOCK = 128
POOL_WINDOWS = (2, 4, 8, 16)
POOL_BUF = max(POOL_WINDOWS) - 1
TOP_K = 6
N_GROUPS = 8
TOPK_GROUPS = 4
ROUTED_SCALE = 2.5
PAST_LEN = 16384

V7X_VMEM_LIMIT_BYTES = 58 * 1024 * 1024
LANE = 128


def _cparams(n_axes):
    return pltpu.CompilerParams(dimension_semantics=("arbitrary",) * n_axes,
                                vmem_limit_bytes=V7X_VMEM_LIMIT_BYTES)


def _pick(n, candidates):
    for c in candidates:
        if c <= n and n % c == 0:
            return c
    return n


def _bdot(a, b):
    return jnp.dot(a.astype(BF16), b.astype(BF16), preferred_element_type=F32)


def _silu(x):
    return x * jax.nn.sigmoid(x)


def _ada_kernel(c_ref, w_ref, b_ref, o_ref):
    s = _silu(c_ref[...])
    o_ref[0] = _bdot(s, w_ref[0]) + b_ref[0]


def _ada(c_all, ada_w, ada_b):
    depth, d, n = ada_w.shape
    bc = c_all.shape[0]
    tn = _pick(n, (1024, 512, 256, 128))
    return pl.pallas_call(
        _ada_kernel,
        grid=(depth, n // tn),
        in_specs=[pl.BlockSpec((bc, d), lambda l, j: (0, 0)),
                  pl.BlockSpec((1, d, tn), lambda l, j: (l, 0, j)),
                  pl.BlockSpec((1, 1, tn), lambda l, j: (l, 0, j))],
        out_specs=pl.BlockSpec((1, bc, tn), lambda l, j: (l, 0, j)),
        out_shape=jax.ShapeDtypeStruct((depth, bc, n), F32),
        compiler_params=_cparams(2),
        name="ada_mod",
    )(c_all, ada_w, ada_b.reshape(depth, 1, n))


def _rms(x, g):
    return x * lax.rsqrt(jnp.mean(x * x, axis=-1, keepdims=True) + EPS) * g


def _modnorm(x, g, shift, scale):
    return _rms(x, g) * (1.0 + scale) + shift


def _mod_spec(mod, tm):
    d = mod.shape[-1]
    if mod.shape[1] == 1:
        return pl.BlockSpec((1, 1, d), lambda b, i: (b, 0, 0))
    return pl.BlockSpec((1, tm, d), lambda b, i: (b, i, 0))


def _norm_mix_kernel(x_ref, g_ref, sh_ref, sc_ref, o_ref):
    o_ref[0] = _modnorm(x_ref[0], g_ref[...], sh_ref[0], sc_ref[0]).astype(o_ref.dtype)


def _norm_mix(x, g, shift, scale):
    bx, l, d = x.shape
    tm = _pick(l, (512, 256, 128, 64, 32, 16))
    return pl.pallas_call(
        _norm_mix_kernel,
        grid=(bx, l // tm),
        in_specs=[pl.BlockSpec((1, tm, d), lambda b, i: (b, i, 0)),
                  pl.BlockSpec((1, d), lambda b, i: (0, 0)),
                  _mod_spec(shift, tm), _mod_spec(scale, tm)],
        out_specs=pl.BlockSpec((1, tm, d), lambda b, i: (b, i, 0)),
        out_shape=jax.ShapeDtypeStruct((bx, l, d), BF16),
        compiler_params=_cparams(2),
        name="norm_mix",
    )(x, g.reshape(1, d), shift, scale)


def _norm_out_kernel(x_ref, g_ref, o_ref):
    o_ref[0] = _rms(x_ref[0], g_ref[...])


def _norm_out(x, g):
    bx, l, d = x.shape
    tm = _pick(l, (512, 256, 128, 64, 32, 16, 8))
    return pl.pallas_call(
        _norm_out_kernel,
        grid=(bx, l // tm),
        in_specs=[pl.BlockSpec((1, tm, d), lambda b, i: (b, i, 0)),
                  pl.BlockSpec((1, d), lambda b, i: (0, 0))],
        out_specs=pl.BlockSpec((1, tm, d), lambda b, i: (b, i, 0)),
        out_shape=jax.ShapeDtypeStruct((bx, l, d), F32),
        compiler_params=_cparams(2),
        name="norm_out",
    )(x, g.reshape(1, d))


def _pack_pair(lo, hi):
    lo_b = lax.bitcast_convert_type(lo.astype(BF16).astype(F32), U32)
    hi_b = lax.bitcast_convert_type(hi.astype(BF16).astype(F32), U32)
    return lax.shift_right_logical(lo_b, jnp.uint32(16)) | (hi_b & jnp.uint32(0xFFFF0000))


def _unpack_pair(word):
    lo = lax.bitcast_convert_type(lax.shift_left(word, jnp.uint32(16)), F32)
    hi = lax.bitcast_convert_type(word & jnp.uint32(0xFFFF0000), F32)
    return lo, hi


def _norm_ffn_kernel(x_ref, g_ref, sh_ref, sc_ref, rwt_ref, up_ref, lg_ref):
    u = _modnorm(x_ref[0], g_ref[...], sh_ref[0], sc_ref[0])
    half = u.shape[1] // 2
    up_ref[...] = _pack_pair(u[:, :half], u[:, half:])
    lg_ref[...] = lax.dot_general(rwt_ref[...], u, (((1,), (1,)), ((), ())),
                                  precision=lax.Precision.HIGHEST, preferred_element_type=F32)


def _norm_ffn(x, g, shift, scale, router_wt):
    bx, l, d = x.shape
    e = router_wt.shape[0]
    tm = _pick(l, (512, 256, 128))
    nl = l // tm
    return pl.pallas_call(
        _norm_ffn_kernel,
        grid=(bx, nl),
        in_specs=[pl.BlockSpec((1, tm, d), lambda b, i: (b, i, 0)),
                  pl.BlockSpec((1, d), lambda b, i: (0, 0)),
                  _mod_spec(shift, tm), _mod_spec(scale, tm),
                  pl.BlockSpec((e, d), lambda b, i: (0, 0))],
        out_specs=[pl.BlockSpec((tm, d // 2), lambda b, i: (b * nl + i, 0)),
                   pl.BlockSpec((e, tm), lambda b, i: (0, b * nl + i))],
        out_shape=[jax.ShapeDtypeStruct((bx * l, d // 2), U32),
                   jax.ShapeDtypeStruct((e, bx * l), F32)],
        compiler_params=_cparams(2),
        name="norm_ffn",
    )(x, g.reshape(1, d), shift, scale, router_wt)


def _mm_kernel(*refs, act, resid):
    if resid:
        x_ref, w_ref, r_ref, g_ref, o_ref, wbf = refs
    else:
        x_ref, w_ref, o_ref, wbf = refs

    @pl.when((pl.program_id(1) == 0) & (pl.program_id(2) == 0))
    def _():
        wbf[...] = w_ref[...].astype(BF16)

    acc = jnp.dot(x_ref[0], wbf[...], preferred_element_type=F32)
    if act == "gelu":
        acc = jax.nn.gelu(acc)
    if resid:
        acc = r_ref[0] + g_ref[0] * acc
    o_ref[0] = acc.astype(o_ref.dtype)


def _mm(x, w, *, col0=0, ncols=None, tn, act=None, resid=None, gate=None, out_dtype=F32, name="mm"):
    bx, l, k = x.shape
    ncols = w.shape[1] - col0 if ncols is None else ncols
    tm = _pick(l, (512, 256, 128, 64, 32, 16))
    nj = ncols // tn
    j0 = col0 // tn
    in_specs = [pl.BlockSpec((1, tm, k), lambda j, b, i: (b, i, 0)),
                pl.BlockSpec((k, tn), lambda j, b, i: (0, j0 + j))]
    args = [x, w]
    if resid is not None:
        in_specs.append(pl.BlockSpec((1, tm, tn), lambda j, b, i: (b, i, j)))
        if gate.shape[1] == 1:
            in_specs.append(pl.BlockSpec((1, 1, tn), lambda j, b, i: (b, 0, j)))
        else:
            in_specs.append(pl.BlockSpec((1, tm, tn), lambda j, b, i: (b, i, j)))
        args += [resid, gate]
    return pl.pallas_call(
        functools.partial(_mm_kernel, act=act, resid=resid is not None),
        grid=(nj, bx, l // tm),
        in_specs=in_specs,
        out_specs=pl.BlockSpec((1, tm, tn), lambda j, b, i: (b, i, j)),
        out_shape=jax.ShapeDtypeStruct((bx, l, ncols), out_dtype),
        scratch_shapes=[pltpu.VMEM((k, tn), BF16)],
        compiler_params=_cparams(3),
        name=name,
    )(*args)


def _lru_coeffs(xc, wrg, wig, brg, big, lam):
    heads = xc.shape[1] // LRU_BLOCK
    rs, igs = [], []
    for h in range(heads):
        xh = xc[:, h * LRU_BLOCK:(h + 1) * LRU_BLOCK].astype(BF16)
        rs.append(jnp.dot(xh, wrg[h], preferred_element_type=F32))
        igs.append(jnp.dot(xh, wig[h], preferred_element_type=F32))
    r = jax.nn.sigmoid(jnp.concatenate(rs, axis=1) + brg)
    ig = jax.nn.sigmoid(jnp.concatenate(igs, axis=1) + big)
    log_a = -LRU_C * r * jax.nn.softplus(-lam)
    a = jnp.exp(log_a)
    b = jnp.sqrt(1.0 - jnp.exp(2.0 * log_a)) * (ig * xc)
    return a, b


def _lru_seq_kernel(rec_ref, gate_ref, h0_ref, cv0_ref, cw_ref, cb_ref, wrg_ref, wig_ref, brg_ref,
                    big_ref, lam_ref, y_ref, hout_ref, ebuf, hcar, wrgb, wigb, *, tc):
    c = pl.program_id(2)

    @pl.when(c == 0)
    def _():
        ebuf[5:8, :] = cv0_ref[0]
        hcar[...] = h0_ref[0]
        wrgb[...] = wrg_ref[...].astype(BF16)
        wigb[...] = wig_ref[...].astype(BF16)

    rec = rec_ref[0]
    ebuf[8:8 + tc, :] = rec
    cw = cw_ref[...]
    xc = cb_ref[...] + ebuf[5:5 + tc, :] * cw[0:1] + ebuf[6:6 + tc, :] * cw[1:2] \
        + ebuf[7:7 + tc, :] * cw[2:3] + rec * cw[3:4]
    a, b = _lru_coeffs(xc, wrgb, wigb, brg_ref[...], big_ref[...], lam_ref[...])

    ct = a.shape[1]
    ng = tc // 8
    a3, b3 = a.reshape(ng, 8, ct), b.reshape(ng, 8, ct)
    row = lax.broadcasted_iota(I32, a3.shape, 1)
    for s in (1, 2, 4):
        m = row >= s
        a_sh = pltpu.roll(a3, s, 1)
        b_sh = pltpu.roll(b3, s, 1)
        b3 = jnp.where(m, a3 * b_sh + b3, b3)
        a3 = jnp.where(m, a3 * a_sh, a3)
    carry = hcar[...]
    hs = []
    for g in range(ng):
        hg = a3[g] * carry + b3[g]
        hs.append(hg)
        carry = hg[7:8, :]
    h = jnp.concatenate(hs, axis=0)
    y_ref[0] = (gate_ref[0] * h).astype(y_ref.dtype)
    hlast = h[tc - 1:tc, :]
    hcar[...] = hlast
    hout_ref[0] = hlast
    ebuf[5:8, :] = rec[tc - 3:tc, :]


def _lru_seq(rec, gate, h0, cv0, conv_w, conv_b, w_rg, w_ig, b_rg, b_ig, lam):
    bx, l, ch = rec.shape
    tc = _pick(l, (256, 128, 64, 32, 16, 8))
    heads = ch // LRU_BLOCK
    hpt = 7 if heads % 7 == 0 else (5 if heads % 5 == 0 else 1)
    ct = hpt * LRU_BLOCK
    nct = ch // ct
    row = lambda a: a.reshape(1, ch)
    cspec = lambda r: pl.BlockSpec((r, ct), lambda b, j, c: (0, j))
    return pl.pallas_call(
        functools.partial(_lru_seq_kernel, tc=tc),
        grid=(bx, nct, l // tc),
        in_specs=[pl.BlockSpec((1, tc, ct), lambda b, j, c: (b, c, j)),
                  pl.BlockSpec((1, tc, ct), lambda b, j, c: (b, c, j)),
                  pl.BlockSpec((1, 1, ct), lambda b, j, c: (b, 0, j)),
                  pl.BlockSpec((1, 3, ct), lambda b, j, c: (b, 0, j)),
                  cspec(4), cspec(1),
                  pl.BlockSpec((hpt, LRU_BLOCK, LRU_BLOCK), lambda b, j, c: (j, 0, 0)),
                  pl.BlockSpec((hpt, LRU_BLOCK, LRU_BLOCK), lambda b, j, c: (j, 0, 0)),
                  cspec(1), cspec(1), cspec(1)],
        out_specs=[pl.BlockSpec((1, tc, ct), lambda b, j, c: (b, c, j)),
                   pl.BlockSpec((1, 1, ct), lambda b, j, c: (b, 0, j))],
        out_shape=[jax.ShapeDtypeStruct((bx, l, ch), BF16),
                   jax.ShapeDtypeStruct((bx, 1, ch), F32)],
        scratch_shapes=[pltpu.VMEM((8 + tc, ct), F32), pltpu.VMEM((1, ct), F32),
                        pltpu.VMEM((hpt, LRU_BLOCK, LRU_BLOCK), BF16),
                        pltpu.VMEM((hpt, LRU_BLOCK, LRU_BLOCK), BF16)],
        compiler_params=_cparams(3),
        name="lru_seq",
    )(rec, gate, h0.reshape(bx, 1, ch), cv0, conv_w, row(conv_b), w_rg, w_ig, row(b_rg), row(b_ig), row(lam))


def _lru_step_kernel(rec_ref, gate_ref, h0_ref, cv0_ref, cw_ref, cb_ref, wrg_ref, wig_ref, brg_ref,
                     big_ref, lam_ref, y_ref, hout_ref):
    rec = rec_ref[...]
    cw = cw_ref[...]
    xc = cb_ref[...] + cv0_ref[0] * cw[0:1] + cv0_ref[1] * cw[1:2] + cv0_ref[2] * cw[2:3] + rec * cw[3:4]
    a, b = _lru_coeffs(xc, wrg_ref[...].astype(BF16), wig_ref[...].astype(BF16),
                       brg_ref[...], big_ref[...], lam_ref[...])
    h = a * h0_ref[...] + b
    y_ref[...] = (gate_ref[...] * h).astype(y_ref.dtype)
    hout_ref[...] = h


def _lru_step(rec, gate, h0, cv0_t, conv_w, conv_b, w_rg, w_ig, b_rg, b_ig, lam):
    bx, ch = rec.shape
    heads = ch // LRU_BLOCK
    hpt = 7 if heads % 7 == 0 else (5 if heads % 5 == 0 else 1)
    ct = hpt * LRU_BLOCK
    row = lambda a: a.reshape(1, ch)
    bspec = pl.BlockSpec((bx, ct), lambda j: (0, j))
    cspec = lambda r: pl.BlockSpec((r, ct), lambda j: (0, j))
    wspec = pl.BlockSpec((hpt, LRU_BLOCK, LRU_BLOCK), lambda j: (j, 0, 0))
    return pl.pallas_call(
        _lru_step_kernel,
        grid=(ch // ct,),
        in_specs=[bspec, bspec, bspec, pl.BlockSpec((3, bx, ct), lambda j: (0, 0, j)),
                  cspec(4), cspec(1), wspec, wspec, cspec(1), cspec(1), cspec(1)],
        out_specs=[bspec, bspec],
        out_shape=[jax.ShapeDtypeStruct((bx, ch), BF16), jax.ShapeDtypeStruct((bx, ch), F32)],
        compiler_params=_cparams(1),
        name="lru_step",
    )(rec, gate, h0, cv0_t, conv_w, row(conv_b), w_rg, w_ig, row(b_rg), row(b_ig), row(lam))


def _pool_seq_kernel(x_ref, g_ref, sh_ref, sc_ref, gt_ref, buf0_ref, pw_ref, ps_ref, o_ref, st_ref,
                     ebuf, pwb, *, tc, pos0):
    c = pl.program_id(1)
    d = x_ref.shape[2]
    gd = d // len(POOL_WINDOWS)

    @pl.when(c == 0)
    def _():
        ebuf[1:16, :] = buf0_ref[0]
        pwb[...] = pw_ref[...].astype(BF16)

    x = x_ref[0]
    u = _modnorm(x, g_ref[...], sh_ref[0], sc_ref[0])
    ebuf[16:16 + tc, :] = u
    pos = pos0 + c * tc + lax.broadcasted_iota(I32, (tc, gd), 0)
    ys = []
    for gi, w in enumerate(POOL_WINDOWS):
        sl = slice(gi * gd, (gi + 1) * gd)
        ssum = u[:, sl]
        for j in range(1, w):
            ssum = ssum + ebuf[16 - j:16 - j + tc, sl]
        cnt = jnp.minimum(w, pos + 1).astype(F32)
        p = ssum / cnt - u[:, sl]
        ys.append(jnp.dot(p.astype(BF16), pwb[gi], preferred_element_type=F32))
    y = jnp.concatenate(ys, axis=1) * ps_ref[...]
    o_ref[0] = x + gt_ref[0] * y
    st_ref[0] = ebuf[tc + 1:tc + 16, :]
    ebuf[0:16, :] = ebuf[tc:tc + 16, :]


def _pool_seq(x, g, shift, scale, gate, buf0, pool_w, pool_scale, pos0):
    bx, l, d = x.shape
    tc = _pick(l, (256, 128, 64, 32, 16))
    ng, gd, _ = pool_w.shape
    mspec = lambda m: (pl.BlockSpec((1, 1, d), lambda b, c: (b, 0, 0)) if m.shape[1] == 1
                       else pl.BlockSpec((1, tc, d), lambda b, c: (b, c, 0)))
    return pl.pallas_call(
        functools.partial(_pool_seq_kernel, tc=tc, pos0=pos0),
        grid=(bx, l // tc),
        in_specs=[pl.BlockSpec((1, tc, d), lambda b, c: (b, c, 0)),
                  pl.BlockSpec((1, d), lambda b, c: (0, 0)),
                  mspec(shift), mspec(scale), mspec(gate),
                  pl.BlockSpec((1, POOL_BUF, d), lambda b, c: (b, 0, 0)),
                  pl.BlockSpec((ng, gd, gd), lambda b, c: (0, 0, 0)),
                  pl.BlockSpec((1, d), lambda b, c: (0, 0))],
        out_specs=[pl.BlockSpec((1, tc, d), lambda b, c: (b, c, 0)),
                   pl.BlockSpec((1, POOL_BUF, d), lambda b, c: (b, 0, 0))],
        out_shape=[jax.ShapeDtypeStruct((bx, l, d), F32),
                   jax.ShapeDtypeStruct((bx, POOL_BUF, d), F32)],
        scratch_shapes=[pltpu.VMEM((16 + tc, d), F32), pltpu.VMEM((ng, gd, gd), BF16)],
        compiler_params=_cparams(2),
        name="pool_seq",
    )(x, g.reshape(1, d), shift, scale, gate, buf0, pool_w, pool_scale.reshape(1, d))


def _pool_step_kernel(x_ref, g_ref, sh_ref, sc_ref, gt_ref, buf_ref, pw_ref, ps_ref, o_ref, u_ref, *, pos0):
    d = x_ref.shape[1]
    gd = d // len(POOL_WINDOWS)
    x = x_ref[...]
    u = _modnorm(x, g_ref[...], sh_ref[...], sc_ref[...])
    ys = []
    for gi, w in enumerate(POOL_WINDOWS):
        sl = slice(gi * gd, (gi + 1) * gd)
        ssum = u[:, sl]
        for j in range(1, w):
            ssum = ssum + buf_ref[POOL_BUF - j, :, sl]
        cnt = float(min(w, pos0 + 1))
        p = ssum / cnt - u[:, sl]
        ys.append(_bdot(p, pw_ref[gi]))
    y = jnp.concatenate(ys, axis=1) * ps_ref[...]
    o_ref[...] = x + gt_ref[...] * y
    u_ref[...] = u


def _pool_step(x, g, shift, scale, gate, buf_t, pool_w, pool_scale, pos0):
    bx, d = x.shape
    ng, gd, _ = pool_w.shape
    full = pl.BlockSpec((bx, d), lambda i: (0, 0))
    vec = pl.BlockSpec((1, d), lambda i: (0, 0))
    return pl.pallas_call(
        functools.partial(_pool_step_kernel, pos0=pos0),
        grid=(1,),
        in_specs=[full, vec, full, full, full,
                  pl.BlockSpec((POOL_BUF, bx, d), lambda i: (0, 0, 0)),
                  pl.BlockSpec((ng, gd, gd), lambda i: (0, 0, 0)), vec],
        out_specs=[full, full],
        out_shape=[jax.ShapeDtypeStruct((bx, d), F32), jax.ShapeDtypeStruct((bx, d), F32)],
        compiler_params=_cparams(1),
        name="pool_step",
    )(x, g.reshape(1, d), shift, scale, gate, buf_t, pool_w, pool_scale.reshape(1, d))


def _first_max(v, idx, n):
    m = jnp.max(v, axis=0, keepdims=True)
    first = jnp.min(jnp.where(v == m, idx, n), axis=0, keepdims=True)
    return jnp.where(idx == first, 1.0, 0.0), m, first


def _stack_rows(rows, dtype):
    n = rows[0].shape[1]
    r8 = lax.broadcasted_iota(I32, (8, n), 0)
    out = jnp.zeros((8, n), dtype)
    for k, r in enumerate(rows):
        out = jnp.where(r8 == k, jnp.broadcast_to(r.astype(dtype), (8, n)), out)
    return out


def _router_kernel(lg_ref, bias_ref, cin_ref, eid_ref, rank_ref, w_ref, cnt_ref, carry, *, tr):
    i = pl.program_id(0)

    @pl.when(i == 0)
    def _():
        carry[...] = cin_ref[:, 0:1]

    ne = lg_ref.shape[0]
    gs = ne // N_GROUPS
    s = jax.nn.sigmoid(lg_ref[...])
    biased = s + bias_ref[...]
    neg = -jnp.inf
    idx8 = lax.broadcasted_iota(I32, (gs, tr), 0)
    gidx = lax.broadcasted_iota(I32, (N_GROUPS, tr), 0)
    gsc = jnp.zeros((N_GROUPS, tr), F32)
    for g in range(N_GROUPS):
        blk = biased[g * gs:(g + 1) * gs, :]
        oh, m1, _ = _first_max(blk, idx8, gs)
        m2 = jnp.max(jnp.where(oh > 0, neg, blk), axis=0, keepdims=True)
        gsc = jnp.where(gidx == g, jnp.broadcast_to(m1 + m2, (N_GROUPS, tr)), gsc)
    gsel = jnp.zeros((N_GROUPS, tr), F32)
    for _ in range(TOPK_GROUPS):
        oh, _, _ = _first_max(gsc, gidx, N_GROUPS)
        gsel = gsel + oh
        gsc = jnp.where(oh > 0, neg, gsc)
    emask = jnp.concatenate([jnp.broadcast_to(gsel[g:g + 1, :], (gs, tr)) for g in range(N_GROUPS)], axis=0)
    masked = jnp.where(emask > 0, biased, neg)
    eidx = lax.broadcasted_iota(I32, (ne, tr), 0)
    ohs, ws, ids = [], [], []
    for _ in range(TOP_K):
        oh, _, first = _first_max(masked, eidx, ne)
        ohs.append(oh)
        ids.append(first)
        ws.append(jnp.sum(oh * s, axis=0, keepdims=True))
        masked = jnp.where(oh > 0, neg, masked)
    wsum = ws[0]
    cm = ohs[0]
    for k in range(1, TOP_K):
        wsum = wsum + ws[k]
        cm = cm + ohs[k]
    tri = (lax.broadcasted_iota(I32, (tr, tr), 0) < lax.broadcasted_iota(I32, (tr, tr), 1))
    prefix = jnp.dot(cm.astype(BF16), jnp.where(tri, 1.0, 0.0).astype(BF16),
                     preferred_element_type=F32) + carry[...]
    rk = [jnp.sum(ohs[k] * prefix, axis=0, keepdims=True) for k in range(TOP_K)]
    rank_ref[0] = _stack_rows(rk, F32).astype(I32)
    eid_ref[0] = _stack_rows(ids, I32)
    w_ref[...] = _stack_rows([w / wsum * ROUTED_SCALE for w in ws], F32)
    carry[...] = carry[...] + jnp.sum(cm, axis=1, keepdims=True)
    cnt_ref[...] = jnp.broadcast_to(carry[...], cnt_ref.shape)


def _router(logits_t, bias, counts_in, tr):
    ne, t = logits_t.shape
    nt = t // tr
    tile = pl.BlockSpec((1, 8, tr), lambda i: (i, 0, 0))
    return pl.pallas_call(
        functools.partial(_router_kernel, tr=tr),
        grid=(nt,),
        in_specs=[pl.BlockSpec((ne, tr), lambda i: (0, i)), pl.BlockSpec((ne, 1), lambda i: (0, 0)),
                  pl.BlockSpec((ne, LANE), lambda i: (0, 0))],
        out_specs=[tile, tile, pl.BlockSpec((8, tr), lambda i: (0, i)),
                   pl.BlockSpec((ne, LANE), lambda i: (0, 0))],
        out_shape=[jax.ShapeDtypeStruct((nt, 8, tr), I32), jax.ShapeDtypeStruct((nt, 8, tr), I32),
                   jax.ShapeDtypeStruct((8, t), F32), jax.ShapeDtypeStruct((ne, LANE), F32)],
        scratch_shapes=[pltpu.VMEM((ne, 1), F32)],
        compiler_params=_cparams(1),
        name="router",
    )(logits_t, bias.reshape(ne, 1), counts_in)


def _dispatch_kernel(zb_ref, desta_ref, destb_ref, xa_ref, xb_ref, xs_ref, zbuf, sem, zsem, *, nta, ne):
    i = pl.program_id(0)
    zrows = zbuf.shape[0]

    @pl.when(i == 0)
    def _():
        zbuf[...] = jnp.zeros_like(zbuf)

        def zero_copy(e):
            return pltpu.make_async_copy(zbuf, xs_ref.at[pl.ds(zb_ref[e] * zrows, zrows), :], zsem)

        def zstart(e, c):
            @pl.when(zb_ref[e] >= 0)
            def _():
                zero_copy(e).start()
            return c

        def zwait(e, c):
            @pl.when(zb_ref[e] >= 0)
            def _():
                zero_copy(e).wait()
            return c

        lax.fori_loop(0, ne, zstart, 0)
        lax.fori_loop(0, ne, zwait, 0)

    def scatter(x_ref, dest_ref):
        def start(t, c):
            for k in range(TOP_K):
                pltpu.make_async_copy(x_ref.at[pl.ds(t, 1), :], xs_ref.at[pl.ds(dest_ref[0, k, t], 1), :],
                                      sem).start(priority=k % 2)
            return c

        def wait(t, c):
            for k in range(TOP_K):
                pltpu.make_async_copy(x_ref.at[pl.ds(0, 1), :], xs_ref.at[pl.ds(0, 1), :], sem).wait()
            return c

        lax.fori_loop(0, x_ref.shape[0], start, 0)
        lax.fori_loop(0, x_ref.shape[0], wait, 0)

    @pl.when(i < nta)
    def _():
        scatter(xa_ref, desta_ref)

    @pl.when(i == nta)
    def _():
        scatter(xb_ref, destb_ref)


def _dispatch(zero_blk, dest_a, packed_a, dest_b, packed_b, r_max, zrows):
    nta, _, tda = dest_a.shape
    _, _, tdb = dest_b.shape
    width = packed_a.shape[1]
    ne = zero_blk.shape[0]
    assert packed_a.shape[0] == nta * tda and packed_b.shape[0] == tdb and dest_b.shape[0] == 1
    return pl.pallas_call(
        functools.partial(_dispatch_kernel, nta=nta, ne=ne),
        grid_spec=pltpu.PrefetchScalarGridSpec(
            num_scalar_prefetch=1,
            grid=(nta + 1,),
            in_specs=[pl.BlockSpec((1, 8, tda), lambda i, zb: (jnp.minimum(i, nta - 1), 0, 0),
                                   memory_space=pltpu.SMEM),
                      pl.BlockSpec((1, 8, tdb), lambda i, zb: (0, 0, 0), memory_space=pltpu.SMEM),
                      pl.BlockSpec((tda, width), lambda i, zb: (jnp.minimum(i, nta - 1), 0)),
                      pl.BlockSpec((tdb, width), lambda i, zb: (0, 0))],
            out_specs=pl.BlockSpec(memory_space=pl.ANY),
            scratch_shapes=[pltpu.VMEM((zrows, width), U32), pltpu.SemaphoreType.DMA, pltpu.SemaphoreType.DMA]),
        out_shape=jax.ShapeDtypeStruct((r_max, width), U32),
        compiler_params=_cparams(1),
        name="dispatch",
    )(zero_blk, dest_a, dest_b, packed_a, packed_b)


EXPERT_F_CHUNK = 256
EXPERT_RING = 3


def _expert_kernel(be_ref, nr_ref, xb_ref, x_ref, wg_hbm, wu_hbm, wd_hbm, o_ref, xbf, gu_ring, wd_ring, sem,
                   *, layer, widths, sub, full):
    del xb_ref
    b = pl.program_id(0)
    n_used = nr_ref[pl.num_programs(0)]
    n = nr_ref[b]
    nsub = x_ref.shape[0] // sub
    nc = len(widths)
    starts = [sum(widths[:c]) for c in range(nc)]

    def copies(blk, c):
        e = be_ref[blk]
        slot, c0, w = c % EXPERT_RING, starts[c], widths[c]
        return (
            pltpu.make_async_copy(wg_hbm.at[layer, e, :, pl.ds(c0, w)], gu_ring.at[slot, :, pl.ds(0, w)], sem.at[slot]),
            pltpu.make_async_copy(wu_hbm.at[layer, e, :, pl.ds(c0, w)], gu_ring.at[slot, :, pl.ds(w, w)], sem.at[slot]),
            pltpu.make_async_copy(wd_hbm.at[layer, e, pl.ds(c0, w), :], wd_ring.at[slot, pl.ds(0, w), :], sem.at[slot]))

    def start(blk, c):
        for cp in copies(blk, c):
            cp.start()

    @pl.when(b < n_used)
    def _():
        @pl.when(b == 0)
        def _():
            start(0, 0)
            start(0, 1)

        def unpack(rows):
            lo, hi = _unpack_pair(x_ref[rows, :])
            xbf[rows, :] = jnp.concatenate([lo, hi], axis=1).astype(BF16)

        for s in range(nsub):
            rows = slice(s * sub, (s + 1) * sub)
            if full:
                unpack(rows)
            else:
                pl.when(s * sub < n)(functools.partial(unpack, rows))

        for c in range(nc):
            nxt, ahead = (c + 2) % nc, (c + 2) // nc
            if ahead == 0:
                start(b, nxt)
            else:
                @pl.when(b + ahead < n_used)
                def _():
                    start(b + ahead, nxt)
            for cp in copies(b, c):
                cp.wait()
            slot, w = c % EXPERT_RING, widths[c]

            def part(r0, nr):
                rows = slice(r0, r0 + nr)
                gu = jnp.dot(xbf[rows, :], gu_ring[slot, :, 0:2 * w].astype(BF16), preferred_element_type=F32)
                h = (_silu(gu[:, :w]) * gu[:, w:]).astype(BF16)
                y = jnp.dot(h, wd_ring[slot, 0:w, :].astype(BF16), preferred_element_type=F32)
                if c == 0:
                    o_ref[rows, :] = y
                else:
                    o_ref[rows, :] += y

            if full:
                part(0, nsub * sub)
            else:
                for k in range(1, nsub + 1):
                    pl.when((n > (k - 1) * sub) & (n <= k * sub))(functools.partial(part, 0, k * sub))


def _experts(layer, block_e, nrows, xblk, xs, w_gate, w_up, w_down, *, tm, sub, n_blocks, full=False):
    _, ne, d, f = w_gate.shape
    width = xs.shape[1]
    tf = min(EXPERT_F_CHUNK, f)
    widths = (tf,) * (f // tf) + ((f % tf,) if f % tf else ())
    assert len(widths) >= EXPERT_RING and len(widths) % EXPERT_RING == 0, widths
    assert (full or tm == 4 * sub) and tm % sub == 0 and nrows.shape[0] == n_blocks + 1
    return pl.pallas_call(
        functools.partial(_expert_kernel, layer=layer, widths=widths, sub=sub, full=full),
        grid_spec=pltpu.PrefetchScalarGridSpec(
            num_scalar_prefetch=3,
            grid=(n_blocks,),
            in_specs=[pl.BlockSpec((tm, width), lambda b, be, nu, xb: (xb[b], 0)),
                      pl.BlockSpec(memory_space=pl.ANY), pl.BlockSpec(memory_space=pl.ANY),
                      pl.BlockSpec(memory_space=pl.ANY)],
            out_specs=pl.BlockSpec((tm, d), lambda b, be, nu, xb: (xb[b], 0)),
            scratch_shapes=[pltpu.VMEM((tm, d), BF16),
                            pltpu.VMEM((EXPERT_RING, d, 2 * tf), F32),
                            pltpu.VMEM((EXPERT_RING, tf, d), F32),
                            pltpu.SemaphoreType.DMA((EXPERT_RING,))]),
        out_shape=jax.ShapeDtypeStruct((n_blocks * tm, d), F32),
        compiler_params=_cparams(1),
        name="experts",
    )(block_e, nrows, xblk, xs, w_gate, w_up, w_down)


def _combine_kernel(dest_ref, x_ref, gt_ref, w_ref, ysh_ref, yb_ref, o_ref, gbuf, sem, *, tmc, per_tile):
    off = (pl.program_id(1) % per_tile) * tmc

    def start(t, c):
        for k in range(TOP_K):
            pltpu.make_async_copy(yb_ref.at[pl.ds(dest_ref[0, k, off + t], 1), :], gbuf.at[k, pl.ds(t, 1), :],
                                  sem).start(priority=k % 2)
        return c

    def wait(t, c):
        for k in range(TOP_K):
            pltpu.make_async_copy(yb_ref.at[pl.ds(0, 1), :], gbuf.at[0, pl.ds(0, 1), :], sem).wait()
        return c

    lax.fori_loop(0, tmc, start, 0)
    lax.fori_loop(0, tmc, wait, 0)
    w = w_ref[...]
    acc = w[:, 0:1] * gbuf[0]
    for k in range(1, TOP_K):
        acc = acc + w[:, k:k + 1] * gbuf[k]
    o_ref[0] = x_ref[0] + gt_ref[0] * (acc + ysh_ref[...])


def _combine(x, gate, dest, w_tok, ysh, yb):
    bx, l, d = x.shape
    tr = dest.shape[2]
    tmc = _pick(tr, (256, 128, 64, 32, 16))
    per_tile = tr // tmc
    nl = l // tmc
    assert l % tr == 0
    gspec = (pl.BlockSpec((1, 1, d), lambda b, i: (b, 0, 0)) if gate.shape[1] == 1
             else pl.BlockSpec((1, tmc, d), lambda b, i: (b, i, 0)))
    return pl.pallas_call(
        functools.partial(_combine_kernel, tmc=tmc, per_tile=per_tile),
        grid=(bx, nl),
        in_specs=[pl.BlockSpec((1, 8, tr), lambda b, i: ((b * nl + i) // per_tile, 0, 0), memory_space=pltpu.SMEM),
                  pl.BlockSpec((1, tmc, d), lambda b, i: (b, i, 0)),
                  gspec,
                  pl.BlockSpec((tmc, 8), lambda b, i: (b * nl + i, 0)),
                  pl.BlockSpec((tmc, d), lambda b, i: (b * nl + i, 0)),
                  pl.BlockSpec(memory_space=pl.ANY)],
        out_specs=pl.BlockSpec((1, tmc, d), lambda b, i: (b, i, 0)),
        out_shape=jax.ShapeDtypeStruct((bx, l, d), F32),
        scratch_shapes=[pltpu.VMEM((TOP_K, tmc, d), F32), pltpu.SemaphoreType.DMA],
        compiler_params=_cparams(2),
        name="combine",
    )(dest, x, gate, w_tok, ysh, yb)


def _expert_block_rows(t_all, ne):
    avg = max(1, t_all * TOP_K // ne)
    if avg >= 512:
        return 1024, 256
    tm = max(64, 2 << avg.bit_length())
    return tm, tm // 4


def _moe(layer, xa, xb, mods_a, mods_b, norm_g, router_w, router_bias, w_gate, w_up, w_down, ws_gate, ws_up, ws_down):
    _, ne, d, _ = w_gate.shape
    ta, tb = xa.shape[0] * xa.shape[1], xb.shape[1]
    t_all = ta + tb
    tm, sub = _expert_block_rows(t_all, ne)
    n_blocks = (t_all * TOP_K + ne * (tm - 1)) // tm

    router_wt = router_w.T
    packed_a, lg_a = _norm_ffn(xa, norm_g, mods_a[0], mods_a[1], router_wt)
    packed_b, lg_b = _norm_ffn(xb, norm_g, mods_b[0], mods_b[1], router_wt)
    tra = _pick(xa.shape[1], (512, 256, 128, 64, 32, 16))
    eid_a, rank_a, w_a, cnt_a = _router(lg_a, router_bias, jnp.zeros((ne, LANE), F32), tra)
    eid_b, rank_b, w_b, cnt = _router(lg_b, router_bias, cnt_a, tb)

    counts = cnt[:, 0].astype(I32)
    nblk = (counts + tm - 1) // tm
    bend = jnp.cumsum(nblk)
    bstart = bend - nblk
    n_used = bend[-1]
    bidx = jnp.arange(n_blocks, dtype=I32)
    last = jnp.maximum(n_used - 1, 0)
    cb = jnp.minimum(bidx, last)
    block_e = jnp.minimum(jnp.sum((bend[None, :] <= cb[:, None]).astype(I32), axis=1), ne - 1)
    eids = jnp.arange(ne, dtype=I32)
    of_block = lambda v: jnp.sum(jnp.where(block_e[:, None] == eids, v, 0), axis=1)
    nrows = jnp.where(bidx < n_used, jnp.clip(of_block(counts) - (bidx - of_block(bstart)) * tm, 0, tm), 0)
    nrows = jnp.concatenate([nrows, n_used.reshape(1)]).astype(I32)
    pstart = (bstart * tm).astype(I32)
    zero_blk = jnp.where(counts > 0, pstart // sub + (counts - 1) // sub, -1).astype(I32)

    def sorted_rows(eid, rank):
        return rank + jnp.sum(jnp.where(eid[..., None] == eids, pstart, 0), axis=-1)

    dest_a, dest_b = sorted_rows(eid_a, rank_a), sorted_rows(eid_b, rank_b)
    xs_sorted = _dispatch(zero_blk, dest_a, packed_a, dest_b, packed_b, n_blocks * tm, sub)
    yb = _experts(layer, block_e, nrows, cb, xs_sorted, w_gate, w_up, w_down, tm=tm, sub=sub, n_blocks=n_blocks)

    outs = []
    for x, packed, mods, dest, w in ((xa, packed_a, mods_a, dest_a, w_a), (xb, packed_b, mods_b, dest_b, w_b)):
        nt = packed.shape[0]
        tms = _pick(nt, (1024, 512, 256, 128, 64, 32, 16))
        nsb = nt // tms
        ysh = _experts(layer, jnp.zeros((nsb,), I32), jnp.array([tms] * nsb + [nsb], I32), jnp.arange(nsb, dtype=I32),
                       packed, ws_gate[:, None], ws_up[:, None], ws_down[:, None],
                       tm=tms, sub=tms, n_blocks=nsb, full=True)
        outs.append(_combine(x, mods[2], dest, w.T, ysh, yb))
    return outs


def kernel(x_prompt, x_sample, state_lru_h, state_lru_conv, state_pool, c_prompt, c_sample, ada_w, ada_b, norm_mix, norm_ffn, norm_out, lru_w_in, lru_conv_w, lru_conv_b, lru_w_rg, lru_b_rg, lru_w_ig, lru_b_ig, lru_lambda, lru_w_out, pool_w, pool_scale, router_w, router_bias, exp_w_gate, exp_w_up, exp_w_down, shared_w_gate, shared_w_up, shared_w_down):
    bp, seq, d = x_prompt.shape
    bs = x_sample.shape[0]
    depth = ada_w.shape[0]
    ch = lru_w_in.shape[2] // 2
    assert x_sample.shape[1] == 1 and seq >= 16

    bc = -(-(bp + bs) // 8) * 8
    c_all = jnp.concatenate([c_prompt, c_sample, jnp.zeros((bc - bp - bs, d), F32)], axis=0)
    mod = _ada(c_all, ada_w, ada_b)

    def mods(i):
        mp = [mod[i, :bp, k * d:(k + 1) * d].reshape(bp, 1, d) for k in range(6)]
        ms = [mod[i, bp:bp + bs, k * d:(k + 1) * d].reshape(1, bs, d) for k in range(6)]
        return mp, ms

    xp = x_prompt
    xs = x_sample.reshape(1, bs, d)
    tn_in = _pick(ch, (896, 640, 512, 384, 256, 128))
    tn_out = _pick(d, (512, 256, 128))
    new_h_p, new_cv_p, new_pb_p, new_h_s, new_cv_s, new_pb_s = [], [], [], [], [], []
    for i in range(depth):
        mp, ms = mods(i)
        j = i // 2
        if i % 2 == 0:
            w_in = lru_w_in[j]
            lru = (lru_conv_w[j], lru_conv_b[j], lru_w_rg[j], lru_w_ig[j], lru_b_rg[j], lru_b_ig[j], lru_lambda[j])
            u = _norm_mix(xp, norm_mix[i], mp[0], mp[1])
            gate = _mm(u, w_in, col0=0, ncols=ch, tn=tn_in, act="gelu", name="lru_in_gate")
            rec = _mm(u, w_in, col0=ch, ncols=ch, tn=tn_in, name="lru_in_rec")
            ypre, h_last = _lru_seq(rec, gate, jnp.zeros((bp, ch), F32), jnp.zeros((bp, 3, ch), F32), *lru)
            xp = _mm(ypre, lru_w_out[j], tn=tn_out, resid=xp, gate=mp[2], name="lru_out")
            new_h_p.append(h_last.reshape(bp, ch))
            new_cv_p.append(rec[:, seq - 3:, :])
            u = _norm_mix(xs, norm_mix[i], ms[0], ms[1])
            gate = _mm(u, w_in, col0=0, ncols=ch, tn=tn_in, act="gelu", name="lru_in_gate")
            rec = _mm(u, w_in, col0=ch, ncols=ch, tn=tn_in, name="lru_in_rec")
            cv0 = state_lru_conv[j]
            ypre, h_new = _lru_step(rec[0], gate[0], state_lru_h[j], cv0.transpose(1, 0, 2), *lru)
            xs = _mm(ypre[None], lru_w_out[j], tn=tn_out, resid=xs, gate=ms[2], name="lru_out")
            new_h_s.append(h_new)
            new_cv_s.append(jnp.concatenate([cv0[:, 1:, :], rec[0][:, None, :]], axis=1))
        else:
            xp, pb = _pool_seq(xp, norm_mix[i], mp[0], mp[1], mp[2], jnp.zeros((bp, POOL_BUF, d), F32),
                               pool_w[j], pool_scale[j], 0)
            new_pb_p.append(pb)
            buf0 = state_pool[j]
            xs2, u_s = _pool_step(xs[0], norm_mix[i], ms[0][0], ms[1][0], ms[2][0], buf0.transpose(1, 0, 2),
                                  pool_w[j], pool_scale[j], PAST_LEN)
            xs = xs2[None]
            new_pb_s.append(jnp.concatenate([buf0[:, 1:, :], u_s[:, None, :]], axis=1))
        xp, xs = _moe(i, xp, xs, (mp[3], mp[4], mp[5]), (ms[3], ms[4], ms[5]), norm_ffn[i],
                      router_w[i], router_bias[i], exp_w_gate, exp_w_up, exp_w_down,
                      shared_w_gate, shared_w_up, shared_w_down)
    y_prompt = _norm_out(xp, norm_out)
    y_sample = _norm_out(xs, norm_out).reshape(bs, 1, d)
    return (y_prompt, y_sample, jnp.stack(new_h_p), jnp.stack(new_cv_p), jnp.stack(new_pb_p),
            jnp.stack(new_h_s), jnp.stack(new_cv_s), jnp.stack(new_pb_s))
```

```python
import functools

import jax
import jax.numpy as jnp
from jax import lax
from jax.experimental import pallas as pl
from jax.experimental.pallas import tpu as pltpu

F32 = jnp.float32
BF16 = jnp.bfloat16
I32 = jnp.int32
U32 = jnp.uint32

EPS = 1e-6
LRU_C = 8.0
LRU_BLOCK = 128
POOL_WINDOWS = (2, 4, 8, 16)
POOL_BUF = max(POOL_WINDOWS) - 1
TOP_K = 6
N_GROUPS = 8
TOPK_GROUPS = 4
ROUTED_SCALE = 2.5
PAST_LEN = 16384

V7X_VMEM_LIMIT_BYTES = 58 * 1024 * 1024
LANE = 128


def _cparams(n_axes):
    return pltpu.CompilerParams(dimension_semantics=("arbitrary",) * n_axes,
                                vmem_limit_bytes=V7X_VMEM_LIMIT_BYTES)


def _pick(n, candidates):
    for c in candidates:
        if c <= n and n % c == 0:
            return c
    return n


def _bdot(a, b):
    return jnp.dot(a.astype(BF16), b.astype(BF16), preferred_element_type=F32)


def _silu(x):
    return x * jax.nn.sigmoid(x)


def _ada_kernel(c_ref, w_ref, b_ref, o_ref):
    s = _silu(c_ref[...])
    o_ref[0] = _bdot(s, w_ref[0]) + b_ref[0]


def _ada(c_all, ada_w, ada_b):
    depth, d, n = ada_w.shape
    bc = c_all.shape[0]
    tn = _pick(n, (1024, 512, 256, 128))
    return pl.pallas_call(
        _ada_kernel,
        grid=(depth, n // tn),
        in_specs=[pl.BlockSpec((bc, d), lambda l, j: (0, 0)),
                  pl.BlockSpec((1, d, tn), lambda l, j: (l, 0, j)),
                  pl.BlockSpec((1, 1, tn), lambda l, j: (l, 0, j))],
        out_specs=pl.BlockSpec((1, bc, tn), lambda l, j: (l, 0, j)),
        out_shape=jax.ShapeDtypeStruct((depth, bc, n), F32),
        compiler_params=_cparams(2),
        name="ada_mod",
    )(c_all, ada_w, ada_b.reshape(depth, 1, n))


def _rms(x, g):
    return x * lax.rsqrt(jnp.mean(x * x, axis=-1, keepdims=True) + EPS) * g


def _modnorm(x, g, shift, scale):
    return _rms(x, g) * (1.0 + scale) + shift


def _mod_spec(mod, tm):
    d = mod.shape[-1]
    if mod.shape[1] == 1:
        return pl.BlockSpec((1, 1, d), lambda b, i: (b, 0, 0))
    return pl.BlockSpec((1, tm, d), lambda b, i: (b, i, 0))


def _norm_mix_kernel(x_ref, g_ref, sh_ref, sc_ref, o_ref):
    o_ref[0] = _modnorm(x_ref[0], g_ref[...], sh_ref[0], sc_ref[0]).astype(o_ref.dtype)


def _norm_mix(x, g, shift, scale):
    bx, l, d = x.shape
    tm = _pick(l, (512, 256, 128, 64, 32, 16))
    return pl.pallas_call(
        _norm_mix_kernel,
        grid=(bx, l // tm),
        in_specs=[pl.BlockSpec((1, tm, d), lambda b, i: (b, i, 0)),
                  pl.BlockSpec((1, d), lambda b, i: (0, 0)),
                  _mod_spec(shift, tm), _mod_spec(scale, tm)],
        out_specs=pl.BlockSpec((1, tm, d), lambda b, i: (b, i, 0)),
        out_shape=jax.ShapeDtypeStruct((bx, l, d), BF16),
        compiler_params=_cparams(2),
        name="norm_mix",
    )(x, g.reshape(1, d), shift, scale)


def _norm_out_kernel(x_ref, g_ref, o_ref):
    o_ref[0] = _rms(x_ref[0], g_ref[...])


def _norm_out(x, g):
    bx, l, d = x.shape
    tm = _pick(l, (512, 256, 128, 64, 32, 16, 8))
    return pl.pallas_call(
        _norm_out_kernel,
        grid=(bx, l // tm),
        in_specs=[pl.BlockSpec((1, tm, d), lambda b, i: (b, i, 0)),
                  pl.BlockSpec((1, d), lambda b, i: (0, 0))],
        out_specs=pl.BlockSpec((1, tm, d), lambda b, i: (b, i, 0)),
        out_shape=jax.ShapeDtypeStruct((bx, l, d), F32),
        compiler_params=_cparams(2),
        name="norm_out",
    )(x, g.reshape(1, d))


def _pack_pair(lo, hi):
    lo_b = lax.bitcast_convert_type(lo.astype(BF16).astype(F32), U32)
    hi_b = lax.bitcast_convert_type(hi.astype(BF16).astype(F32), U32)
    return lax.shift_right_logical(lo_b, jnp.uint32(16)) | (hi_b & jnp.uint32(0xFFFF0000))


def _unpack_pair(word):
    lo = lax.bitcast_convert_type(lax.shift_left(word, jnp.uint32(16)), F32)
    hi = lax.bitcast_convert_type(word & jnp.uint32(0xFFFF0000), F32)
    return lo, hi


def _norm_ffn_kernel(x_ref, g_ref, sh_ref, sc_ref, rwt_ref, up_ref, lg_ref):
    u = _modnorm(x_ref[0], g_ref[...], sh_ref[0], sc_ref[0])
    half = u.shape[1] // 2
    up_ref[...] = _pack_pair(u[:, :half], u[:, half:])
    lg_ref[...] = lax.dot_general(rwt_ref[...], u, (((1,), (1,)), ((), ())),
                                  precision=lax.Precision.HIGHEST, preferred_element_type=F32)


def _norm_ffn(x, g, shift, scale, router_wt):
    bx, l, d = x.shape
    e = router_wt.shape[0]
    tm = _pick(l, (512, 256, 128))
    nl = l // tm
    return pl.pallas_call(
        _norm_ffn_kernel,
        grid=(bx, nl),
        in_specs=[pl.BlockSpec((1, tm, d), lambda b, i: (b, i, 0)),
                  pl.BlockSpec((1, d), lambda b, i: (0, 0)),
                  _mod_spec(shift, tm), _mod_spec(scale, tm),
                  pl.BlockSpec((e, d), lambda b, i: (0, 0))],
        out_specs=[pl.BlockSpec((tm, d // 2), lambda b, i: (b * nl + i, 0)),
                   pl.BlockSpec((e, tm), lambda b, i: (0, b * nl + i))],
        out_shape=[jax.ShapeDtypeStruct((bx * l, d // 2), U32),
                   jax.ShapeDtypeStruct((e, bx * l), F32)],
        compiler_params=_cparams(2),
        name="norm_ffn",
    )(x, g.reshape(1, d), shift, scale, router_wt)


def _mm_kernel(*refs, act, resid):
    if resid:
        x_ref, w_ref, r_ref, g_ref, o_ref, wbf = refs
    else:
        x_ref, w_ref, o_ref, wbf = refs

    @pl.when((pl.program_id(1) == 0) & (pl.program_id(2) == 0))
    def _():
        wbf[...] = w_ref[...].astype(BF16)

    acc = jnp.dot(x_ref[0], wbf[...], preferred_element_type=F32)
    if act == "gelu":
        acc = jax.nn.gelu(acc)
    if resid:
        acc = r_ref[0] + g_ref[0] * acc
    o_ref[0] = acc.astype(o_ref.dtype)


def _mm(x, w, *, col0=0, ncols=None, tn, act=None, resid=None, gate=None, out_dtype=F32, name="mm"):
    bx, l, k = x.shape
    ncols = w.shape[1] - col0 if ncols is None else ncols
    tm = _pick(l, (1024, 512, 256, 128, 64, 32, 16))
    nj = ncols // tn
    j0 = col0 // tn
    in_specs = [pl.BlockSpec((1, tm, k), lambda j, b, i: (b, i, 0)),
                pl.BlockSpec((k, tn), lambda j, b, i: (0, j0 + j))]
    args = [x, w]
    if resid is not None:
        in_specs.append(pl.BlockSpec((1, tm, tn), lambda j, b, i: (b, i, j)))
        if gate.shape[1] == 1:
            in_specs.append(pl.BlockSpec((1, 1, tn), lambda j, b, i: (b, 0, j)))
        else:
            in_specs.append(pl.BlockSpec((1, tm, tn), lambda j, b, i: (b, i, j)))
        args += [resid, gate]
    return pl.pallas_call(
        functools.partial(_mm_kernel, act=act, resid=resid is not None),
        grid=(nj, bx, l // tm),
        in_specs=in_specs,
        out_specs=pl.BlockSpec((1, tm, tn), lambda j, b, i: (b, i, j)),
        out_shape=jax.ShapeDtypeStruct((bx, l, ncols), out_dtype),
        scratch_shapes=[pltpu.VMEM((k, tn), BF16)],
        compiler_params=_cparams(3),
        name=name,
    )(*args)


def _lru_coeffs(xc, wrg, wig, brg, big, lam):
    heads = xc.shape[1] // LRU_BLOCK
    rs, igs = [], []
    for h in range(heads):
        xh = xc[:, h * LRU_BLOCK:(h + 1) * LRU_BLOCK].astype(BF16)
        rs.append(jnp.dot(xh, wrg[h], preferred_element_type=F32))
        igs.append(jnp.dot(xh, wig[h], preferred_element_type=F32))
    r = jax.nn.sigmoid(jnp.concatenate(rs, axis=1) + brg)
    ig = jax.nn.sigmoid(jnp.concatenate(igs, axis=1) + big)
    log_a = -LRU_C * r * jax.nn.softplus(-lam)
    a = jnp.exp(log_a)
    b = jnp.sqrt(1.0 - jnp.exp(2.0 * log_a)) * (ig * xc)
    return a, b


def _lru_seq_kernel(rec_ref, gate_ref, h0_ref, cv0_ref, cw_ref, cb_ref, wrg_ref, wig_ref, brg_ref,
                    big_ref, lam_ref, y_ref, hout_ref, ebuf, hcar, wrgb, wigb, *, tc):
    c = pl.program_id(2)

    @pl.when(c == 0)
    def _():
        ebuf[5:8, :] = cv0_ref[0]
        hcar[...] = h0_ref[0]
        wrgb[...] = wrg_ref[...].astype(BF16)
        wigb[...] = wig_ref[...].astype(BF16)

    rec = rec_ref[0]
    ebuf[8:8 + tc, :] = rec
    cw = cw_ref[...]
    xc = cb_ref[...] + ebuf[5:5 + tc, :] * cw[0:1] + ebuf[6:6 + tc, :] * cw[1:2] \
        + ebuf[7:7 + tc, :] * cw[2:3] + rec * cw[3:4]
    a, b = _lru_coeffs(xc, wrgb, wigb, brg_ref[...], big_ref[...], lam_ref[...])

    ct = a.shape[1]
    ng = tc // 8
    a3, b3 = a.reshape(ng, 8, ct), b.reshape(ng, 8, ct)
    row = lax.broadcasted_iota(I32, a3.shape, 1)
    for s in (1, 2, 4):
        m = row >= s
        a_sh = pltpu.roll(a3, s, 1)
        b_sh = pltpu.roll(b3, s, 1)
        b3 = jnp.where(m, a3 * b_sh + b3, b3)
        a3 = jnp.where(m, a3 * a_sh, a3)
    carry = hcar[...]
    hs = []
    for g in range(ng):
        hg = a3[g] * carry + b3[g]
        hs.append(hg)
        carry = hg[7:8, :]
    h = jnp.concatenate(hs, axis=0)
    y_ref[0] = (gate_ref[0] * h).astype(y_ref.dtype)
    hlast = h[tc - 1:tc, :]
    hcar[...] = hlast
    hout_ref[0] = hlast
    ebuf[5:8, :] = rec[tc - 3:tc, :]


def _lru_seq(rec, gate, h0, cv0, conv_w, conv_b, w_rg, w_ig, b_rg, b_ig, lam):
    bx, l, ch = rec.shape
    tc = _pick(l, (256, 128, 64, 32, 16, 8))
    heads = ch // LRU_BLOCK
    hpt = 7 if heads % 7 == 0 else (5 if heads % 5 == 0 else 1)
    ct = hpt * LRU_BLOCK
    nct = ch // ct
    row = lambda a: a.reshape(1, ch)
    cspec = lambda r: pl.BlockSpec((r, ct), lambda b, j, c: (0, j))
    return pl.pallas_call(
        functools.partial(_lru_seq_kernel, tc=tc),
        grid=(bx, nct, l // tc),
        in_specs=[pl.BlockSpec((1, tc, ct), lambda b, j, c: (b, c, j)),
                  pl.BlockSpec((1, tc, ct), lambda b, j, c: (b, c, j)),
                  pl.BlockSpec((1, 1, ct), lambda b, j, c: (b, 0, j)),
                  pl.BlockSpec((1, 3, ct), lambda b, j, c: (b, 0, j)),
                  cspec(4), cspec(1),
                  pl.BlockSpec((hpt, LRU_BLOCK, LRU_BLOCK), lambda b, j, c: (j, 0, 0)),
                  pl.BlockSpec((hpt, LRU_BLOCK, LRU_BLOCK), lambda b, j, c: (j, 0, 0)),
                  cspec(1), cspec(1), cspec(1)],
        out_specs=[pl.BlockSpec((1, tc, ct), lambda b, j, c: (b, c, j)),
                   pl.BlockSpec((1, 1, ct), lambda b, j, c: (b, 0, j))],
        out_shape=[jax.ShapeDtypeStruct((bx, l, ch), BF16),
                   jax.ShapeDtypeStruct((bx, 1, ch), F32)],
        scratch_shapes=[pltpu.VMEM((8 + tc, ct), F32), pltpu.VMEM((1, ct), F32),
                        pltpu.VMEM((hpt, LRU_BLOCK, LRU_BLOCK), BF16),
                        pltpu.VMEM((hpt, LRU_BLOCK, LRU_BLOCK), BF16)],
        compiler_params=_cparams(3),
        name="lru_seq",
    )(rec, gate, h0.reshape(bx, 1, ch), cv0, conv_w, row(conv_b), w_rg, w_ig, row(b_rg), row(b_ig), row(lam))


def _lru_step_kernel(rec_ref, gate_ref, h0_ref, cv0_ref, cw_ref, cb_ref, wrg_ref, wig_ref, brg_ref,
                     big_ref, lam_ref, y_ref, hout_ref):
    rec = rec_ref[...]
    cw = cw_ref[...]
    xc = cb_ref[...] + cv0_ref[0] * cw[0:1] + cv0_ref[1] * cw[1:2] + cv0_ref[2] * cw[2:3] + rec * cw[3:4]
    a, b = _lru_coeffs(xc, wrg_ref[...].astype(BF16), wig_ref[...].astype(BF16),
                       brg_ref[...], big_ref[...], lam_ref[...])
    h = a * h0_ref[...] + b
    y_ref[...] = (gate_ref[...] * h).astype(y_ref.dtype)
    hout_ref[...] = h


def _lru_step(rec, gate, h0, cv0_t, conv_w, conv_b, w_rg, w_ig, b_rg, b_ig, lam):
    bx, ch = rec.shape
    heads = ch // LRU_BLOCK
    hpt = 7 if heads % 7 == 0 else (5 if heads % 5 == 0 else 1)
    ct = hpt * LRU_BLOCK
    row = lambda a: a.reshape(1, ch)
    bspec = pl.BlockSpec((bx, ct), lambda j: (0, j))
    cspec = lambda r: pl.BlockSpec((r, ct), lambda j: (0, j))
    wspec = pl.BlockSpec((hpt, LRU_BLOCK, LRU_BLOCK), lambda j: (j, 0, 0))
    return pl.pallas_call(
        _lru_step_kernel,
        grid=(ch // ct,),
        in_specs=[bspec, bspec, bspec, pl.BlockSpec((3, bx, ct), lambda j: (0, 0, j)),
                  cspec(4), cspec(1), wspec, wspec, cspec(1), cspec(1), cspec(1)],
        out_specs=[bspec, bspec],
        out_shape=[jax.ShapeDtypeStruct((bx, ch), BF16), jax.ShapeDtypeStruct((bx, ch), F32)],
        compiler_params=_cparams(1),
        name="lru_step",
    )(rec, gate, h0, cv0_t, conv_w, row(conv_b), w_rg, w_ig, row(b_rg), row(b_ig), row(lam))


def _pool_seq_kernel(x_ref, g_ref, sh_ref, sc_ref, gt_ref, buf0_ref, pw_ref, ps_ref, o_ref, st_ref,
                     ebuf, pwb, *, tc, pos0):
    c = pl.program_id(1)
    d = x_ref.shape[2]
    gd = d // len(POOL_WINDOWS)

    @pl.when(c == 0)
    def _():
        ebuf[1:16, :] = buf0_ref[0]
        pwb[...] = pw_ref[...].astype(BF16)

    x = x_ref[0]
    u = _modnorm(x, g_ref[...], sh_ref[0], sc_ref[0])
    ebuf[16:16 + tc, :] = u
    pos = pos0 + c * tc + lax.broadcasted_iota(I32, (tc, gd), 0)
    ys = []
    for gi, w in enumerate(POOL_WINDOWS):
        sl = slice(gi * gd, (gi + 1) * gd)
        ssum = u[:, sl]
        for j in range(1, w):
            ssum = ssum + ebuf[16 - j:16 - j + tc, sl]
        cnt = jnp.minimum(w, pos + 1).astype(F32)
        p = ssum / cnt - u[:, sl]
        ys.append(jnp.dot(p.astype(BF16), pwb[gi], preferred_element_type=F32))
    y = jnp.concatenate(ys, axis=1) * ps_ref[...]
    o_ref[0] = x + gt_ref[0] * y
    st_ref[0] = ebuf[tc + 1:tc + 16, :]
    ebuf[0:16, :] = ebuf[tc:tc + 16, :]


def _pool_seq(x, g, shift, scale, gate, buf0, pool_w, pool_scale, pos0):
    bx, l, d = x.shape
    tc = _pick(l, (256, 128, 64, 32, 16))
    ng, gd, _ = pool_w.shape
    mspec = lambda m: (pl.BlockSpec((1, 1, d), lambda b, c: (b, 0, 0)) if m.shape[1] == 1
                       else pl.BlockSpec((1, tc, d), lambda b, c: (b, c, 0)))
    return pl.pallas_call(
        functools.partial(_pool_seq_kernel, tc=tc, pos0=pos0),
        grid=(bx, l // tc),
        in_specs=[pl.BlockSpec((1, tc, d), lambda b, c: (b, c, 0)),
                  pl.BlockSpec((1, d), lambda b, c: (0, 0)),
                  mspec(shift), mspec(scale), mspec(gate),
                  pl.BlockSpec((1, POOL_BUF, d), lambda b, c: (b, 0, 0)),
                  pl.BlockSpec((ng, gd, gd), lambda b, c: (0, 0, 0)),
                  pl.BlockSpec((1, d), lambda b, c: (0, 0))],
        out_specs=[pl.BlockSpec((1, tc, d), lambda b, c: (b, c, 0)),
                   pl.BlockSpec((1, POOL_BUF, d), lambda b, c: (b, 0, 0))],
        out_shape=[jax.ShapeDtypeStruct((bx, l, d), F32),
                   jax.ShapeDtypeStruct((bx, POOL_BUF, d), F32)],
        scratch_shapes=[pltpu.VMEM((16 + tc, d), F32), pltpu.VMEM((ng, gd, gd), BF16)],
        compiler_params=_cparams(2),
        name="pool_seq",
    )(x, g.reshape(1, d), shift, scale, gate, buf0, pool_w, pool_scale.reshape(1, d))


def _pool_step_kernel(x_ref, g_ref, sh_ref, sc_ref, gt_ref, buf_ref, pw_ref, ps_ref, o_ref, u_ref, *, pos0):
    d = x_ref.shape[1]
    gd = d // len(POOL_WINDOWS)
    x = x_ref[...]
    u = _modnorm(x, g_ref[...], sh_ref[...], sc_ref[...])
    ys = []
    for gi, w in enumerate(POOL_WINDOWS):
        sl = slice(gi * gd, (gi + 1) * gd)
        ssum = u[:, sl]
        for j in range(1, w):
            ssum = ssum + buf_ref[POOL_BUF - j, :, sl]
        cnt = float(min(w, pos0 + 1))
        p = ssum / cnt - u[:, sl]
        ys.append(_bdot(p, pw_ref[gi]))
    y = jnp.concatenate(ys, axis=1) * ps_ref[...]
    o_ref[...] = x + gt_ref[...] * y
    u_ref[...] = u


def _pool_step(x, g, shift, scale, gate, buf_t, pool_w, pool_scale, pos0):
    bx, d = x.shape
    ng, gd, _ = pool_w.shape
    full = pl.BlockSpec((bx, d), lambda i: (0, 0))
    vec = pl.BlockSpec((1, d), lambda i: (0, 0))
    return pl.pallas_call(
        functools.partial(_pool_step_kernel, pos0=pos0),
        grid=(1,),
        in_specs=[full, vec, full, full, full,
                  pl.BlockSpec((POOL_BUF, bx, d), lambda i: (0, 0, 0)),
                  pl.BlockSpec((ng, gd, gd), lambda i: (0, 0, 0)), vec],
        out_specs=[full, full],
        out_shape=[jax.ShapeDtypeStruct((bx, d), F32), jax.ShapeDtypeStruct((bx, d), F32)],
        compiler_params=_cparams(1),
        name="pool_step",
    )(x, g.reshape(1, d), shift, scale, gate, buf_t, pool_w, pool_scale.reshape(1, d))


def _first_max(v, idx, n):
    m = jnp.max(v, axis=0, keepdims=True)
    first = jnp.min(jnp.where(v == m, idx, n), axis=0, keepdims=True)
    return jnp.where(idx == first, 1.0, 0.0), m, first


def _stack_rows(rows, dtype):
    n = rows[0].shape[1]
    r8 = lax.broadcasted_iota(I32, (8, n), 0)
    out = jnp.zeros((8, n), dtype)
    for k, r in enumerate(rows):
        out = jnp.where(r8 == k, jnp.broadcast_to(r.astype(dtype), (8, n)), out)
    return out


def _router_kernel(lg_ref, bias_ref, cin_ref, eid_ref, rank_ref, w_ref, cnt_ref, carry, *, tr):
    i = pl.program_id(0)

    @pl.when(i == 0)
    def _():
        carry[...] = cin_ref[:, 0:1]

    ne = lg_ref.shape[0]
    gs = ne // N_GROUPS
    s = jax.nn.sigmoid(lg_ref[...])
    biased = s + bias_ref[...]
    neg = -jnp.inf
    idx8 = lax.broadcasted_iota(I32, (gs, tr), 0)
    gidx = lax.broadcasted_iota(I32, (N_GROUPS, tr), 0)
    gsc = jnp.zeros((N_GROUPS, tr), F32)
    for g in range(N_GROUPS):
        blk = biased[g * gs:(g + 1) * gs, :]
        oh, m1, _ = _first_max(blk, idx8, gs)
        m2 = jnp.max(jnp.where(oh > 0, neg, blk), axis=0, keepdims=True)
        gsc = jnp.where(gidx == g, jnp.broadcast_to(m1 + m2, (N_GROUPS, tr)), gsc)
    gsel = jnp.zeros((N_GROUPS, tr), F32)
    for _ in range(TOPK_GROUPS):
        oh, _, _ = _first_max(gsc, gidx, N_GROUPS)
        gsel = gsel + oh
        gsc = jnp.where(oh > 0, neg, gsc)
    emask = jnp.concatenate([jnp.broadcast_to(gsel[g:g + 1, :], (gs, tr)) for g in range(N_GROUPS)], axis=0)
    masked = jnp.where(emask > 0, biased, neg)
    eidx = lax.broadcasted_iota(I32, (ne, tr), 0)
    ohs, ws, ids = [], [], []
    for _ in range(TOP_K):
        oh, _, first = _first_max(masked, eidx, ne)
        ohs.append(oh)
        ids.append(first)
        ws.append(jnp.sum(oh * s, axis=0, keepdims=True))
        masked = jnp.where(oh > 0, neg, masked)
    wsum = ws[0]
    cm = ohs[0]
    for k in range(1, TOP_K):
        wsum = wsum + ws[k]
        cm = cm + ohs[k]
    tri = (lax.broadcasted_iota(I32, (tr, tr), 0) < lax.broadcasted_iota(I32, (tr, tr), 1))
    prefix = jnp.dot(cm.astype(BF16), jnp.where(tri, 1.0, 0.0).astype(BF16),
                     preferred_element_type=F32) + carry[...]
    rk = [jnp.sum(ohs[k] * prefix, axis=0, keepdims=True) for k in range(TOP_K)]
    rank_ref[0] = _stack_rows(rk, F32).astype(I32)
    eid_ref[0] = _stack_rows(ids, I32)
    w_ref[...] = _stack_rows([w / wsum * ROUTED_SCALE for w in ws], F32)
    carry[...] = carry[...] + jnp.sum(cm, axis=1, keepdims=True)
    cnt_ref[...] = jnp.broadcast_to(carry[...], cnt_ref.shape)


def _router(logits_t, bias, counts_in, tr):
    ne, t = logits_t.shape
    nt = t // tr
    tile = pl.BlockSpec((1, 8, tr), lambda i: (i, 0, 0))
    return pl.pallas_call(
        functools.partial(_router_kernel, tr=tr),
        grid=(nt,),
        in_specs=[pl.BlockSpec((ne, tr), lambda i: (0, i)), pl.BlockSpec((ne, 1), lambda i: (0, 0)),
                  pl.BlockSpec((ne, LANE), lambda i: (0, 0))],
        out_specs=[tile, tile, pl.BlockSpec((8, tr), lambda i: (0, i)),
                   pl.BlockSpec((ne, LANE), lambda i: (0, 0))],
        out_shape=[jax.ShapeDtypeStruct((nt, 8, tr), I32), jax.ShapeDtypeStruct((nt, 8, tr), I32),
                   jax.ShapeDtypeStruct((8, t), F32), jax.ShapeDtypeStruct((ne, LANE), F32)],
        scratch_shapes=[pltpu.VMEM((ne, 1), F32)],
        compiler_params=_cparams(1),
        name="router",
    )(logits_t, bias.reshape(ne, 1), counts_in)


def _dispatch_kernel(zb_ref, desta_ref, destb_ref, xa_ref, xb_ref, xs_ref, zbuf, sem, zsem, *, nta, ne):
    i = pl.program_id(0)
    zrows = zbuf.shape[0]

    @pl.when(i == 0)
    def _():
        zbuf[...] = jnp.zeros_like(zbuf)

        def zero_copy(e):
            return pltpu.make_async_copy(zbuf, xs_ref.at[pl.ds(zb_ref[e] * zrows, zrows), :], zsem)

        def zstart(e, c):
            @pl.when(zb_ref[e] >= 0)
            def _():
                zero_copy(e).start()
            return c

        def zwait(e, c):
            @pl.when(zb_ref[e] >= 0)
            def _():
                zero_copy(e).wait()
            return c

        lax.fori_loop(0, ne, zstart, 0)
        lax.fori_loop(0, ne, zwait, 0)

    def scatter(x_ref, dest_ref):
        def start(t, c):
            for k in range(TOP_K):
                pltpu.make_async_copy(x_ref.at[pl.ds(t, 1), :], xs_ref.at[pl.ds(dest_ref[0, k, t], 1), :],
                                      sem).start(priority=k % 2)
            return c

        def wait(t, c):
            for k in range(TOP_K):
                pltpu.make_async_copy(x_ref.at[pl.ds(0, 1), :], xs_ref.at[pl.ds(0, 1), :], sem).wait()
            return c

        lax.fori_loop(0, x_ref.shape[0], start, 0)
        lax.fori_loop(0, x_ref.shape[0], wait, 0)

    @pl.when(i < nta)
    def _():
        scatter(xa_ref, desta_ref)

    @pl.when(i == nta)
    def _():
        scatter(xb_ref, destb_ref)


def _dispatch(zero_blk, dest_a, packed_a, dest_b, packed_b, r_max, zrows):
    nta, _, tda = dest_a.shape
    _, _, tdb = dest_b.shape
    width = packed_a.shape[1]
    ne = zero_blk.shape[0]
    assert packed_a.shape[0] == nta * tda and packed_b.shape[0] == tdb and dest_b.shape[0] == 1
    return pl.pallas_call(
        functools.partial(_dispatch_kernel, nta=nta, ne=ne),
        grid_spec=pltpu.PrefetchScalarGridSpec(
            num_scalar_prefetch=1,
            grid=(nta + 1,),
            in_specs=[pl.BlockSpec((1, 8, tda), lambda i, zb: (jnp.minimum(i, nta - 1), 0, 0),
                                   memory_space=pltpu.SMEM),
                      pl.BlockSpec((1, 8, tdb), lambda i, zb: (0, 0, 0), memory_space=pltpu.SMEM),
                      pl.BlockSpec((tda, width), lambda i, zb: (jnp.minimum(i, nta - 1), 0)),
                      pl.BlockSpec((tdb, width), lambda i, zb: (0, 0))],
            out_specs=pl.BlockSpec(memory_space=pl.ANY),
            scratch_shapes=[pltpu.VMEM((zrows, width), U32), pltpu.SemaphoreType.DMA, pltpu.SemaphoreType.DMA]),
        out_shape=jax.ShapeDtypeStruct((r_max, width), U32),
        compiler_params=_cparams(1),
        name="dispatch",
    )(zero_blk, dest_a, dest_b, packed_a, packed_b)


EXPERT_F_CHUNK = 256
EXPERT_RING = 3


def _expert_kernel(be_ref, nr_ref, xb_ref, x_ref, wg_hbm, wu_hbm, wd_hbm, o_ref, xbf, gu_ring, wd_ring, sem,
                   *, layer, widths, sub, full):
    del xb_ref
    b = pl.program_id(0)
    n_used = nr_ref[pl.num_programs(0)]
    n = nr_ref[b]
    nsub = x_ref.shape[0] // sub
    nc = len(widths)
    starts = [sum(widths[:c]) for c in range(nc)]

    def copies(blk, c):
        e = be_ref[blk]
        slot, c0, w = c % EXPERT_RING, starts[c], widths[c]
        return (
            pltpu.make_async_copy(wg_hbm.at[layer, e, :, pl.ds(c0, w)], gu_ring.at[slot, :, pl.ds(0, w)], sem.at[slot]),
            pltpu.make_async_copy(wu_hbm.at[layer, e, :, pl.ds(c0, w)], gu_ring.at[slot, :, pl.ds(w, w)], sem.at[slot]),
            pltpu.make_async_copy(wd_hbm.at[layer, e, pl.ds(c0, w), :], wd_ring.at[slot, pl.ds(0, w), :], sem.at[slot]))

    def start(blk, c):
        for cp in copies(blk, c):
            cp.start()

    @pl.when(b < n_used)
    def _():
        @pl.when(b == 0)
        def _():
            start(0, 0)
            start(0, 1)

        def unpack(rows):
            lo, hi = _unpack_pair(x_ref[rows, :])
            xbf[rows, :] = jnp.concatenate([lo, hi], axis=1).astype(BF16)

        for s in range(nsub):
            rows = slice(s * sub, (s + 1) * sub)
            if full:
                unpack(rows)
            else:
                pl.when(s * sub < n)(functools.partial(unpack, rows))

        for c in range(nc):
            nxt, ahead = (c + 2) % nc, (c + 2) // nc
            if ahead == 0:
                start(b, nxt)
            else:
                @pl.when(b + ahead < n_used)
                def _():
                    start(b + ahead, nxt)
            for cp in copies(b, c):
                cp.wait()
            slot, w = c % EXPERT_RING, widths[c]

            def part(r0, nr):
                rows = slice(r0, r0 + nr)
                gu = jnp.dot(xbf[rows, :], gu_ring[slot, :, 0:2 * w].astype(BF16), preferred_element_type=F32)
                h = (_silu(gu[:, :w]) * gu[:, w:]).astype(BF16)
                y = jnp.dot(h, wd_ring[slot, 0:w, :].astype(BF16), preferred_element_type=F32)
                if c == 0:
                    o_ref[rows, :] = y
                else:
                    o_ref[rows, :] += y

            if full:
                part(0, nsub * sub)
            else:
                for r0 in (0, 2 * sub):
                    pl.when(n > r0 + sub)(functools.partial(part, r0, 2 * sub))
                    pl.when((n > r0) & (n <= r0 + sub))(functools.partial(part, r0, sub))


def _experts(layer, block_e, nrows, xblk, xs, w_gate, w_up, w_down, *, tm, sub, n_blocks, full=False):
    _, ne, d, f = w_gate.shape
    width = xs.shape[1]
    tf = min(EXPERT_F_CHUNK, f)
    widths = (tf,) * (f // tf) + ((f % tf,) if f % tf else ())
    assert len(widths) >= EXPERT_RING and len(widths) % EXPERT_RING == 0, widths
    assert (full or tm == 4 * sub) and tm % sub == 0 and nrows.shape[0] == n_blocks + 1
    return pl.pallas_call(
        functools.partial(_expert_kernel, layer=layer, widths=widths, sub=sub, full=full),
        grid_spec=pltpu.PrefetchScalarGridSpec(
            num_scalar_prefetch=3,
            grid=(n_blocks,),
            in_specs=[pl.BlockSpec((tm, width), lambda b, be, nu, xb: (xb[b], 0)),
                      pl.BlockSpec(memory_space=pl.ANY), pl.BlockSpec(memory_space=pl.ANY),
                      pl.BlockSpec(memory_space=pl.ANY)],
            out_specs=pl.BlockSpec((tm, d), lambda b, be, nu, xb: (xb[b], 0)),
            scratch_shapes=[pltpu.VMEM((tm, d), BF16),
                            pltpu.VMEM((EXPERT_RING, d, 2 * tf), F32),
                            pltpu.VMEM((EXPERT_RING, tf, d), F32),
                            pltpu.SemaphoreType.DMA((EXPERT_RING,))]),
        out_shape=jax.ShapeDtypeStruct((n_blocks * tm, d), F32),
        compiler_params=_cparams(1),
        name="experts",
    )(block_e, nrows, xblk, xs, w_gate, w_up, w_down)


def _combine_kernel(dest_ref, x_ref, gt_ref, w_ref, ysh_ref, yb_ref, o_ref, gbuf, sem, *, tmc, per_tile):
    off = (pl.program_id(1) % per_tile) * tmc

    def start(t, c):
        for k in range(TOP_K):
            pltpu.make_async_copy(yb_ref.at[pl.ds(dest_ref[0, k, off + t], 1), :], gbuf.at[k, pl.ds(t, 1), :],
                                  sem).start(priority=k % 2)
        return c

    def wait(t, c):
        for k in range(TOP_K):
            pltpu.make_async_copy(yb_ref.at[pl.ds(0, 1), :], gbuf.at[0, pl.ds(0, 1), :], sem).wait()
        return c

    lax.fori_loop(0, tmc, start, 0)
    lax.fori_loop(0, tmc, wait, 0)
    w = w_ref[...]
    acc = w[:, 0:1] * gbuf[0]
    for k in range(1, TOP_K):
        acc = acc + w[:, k:k + 1] * gbuf[k]
    o_ref[0] = x_ref[0] + gt_ref[0] * (acc + ysh_ref[...])


def _combine(x, gate, dest, w_tok, ysh, yb):
    bx, l, d = x.shape
    tr = dest.shape[2]
    tmc = _pick(tr, (256, 128, 64, 32, 16))
    per_tile = tr // tmc
    nl = l // tmc
    assert l % tr == 0
    gspec = (pl.BlockSpec((1, 1, d), lambda b, i: (b, 0, 0)) if gate.shape[1] == 1
             else pl.BlockSpec((1, tmc, d), lambda b, i: (b, i, 0)))
    return pl.pallas_call(
        functools.partial(_combine_kernel, tmc=tmc, per_tile=per_tile),
        grid=(bx, nl),
        in_specs=[pl.BlockSpec((1, 8, tr), lambda b, i: ((b * nl + i) // per_tile, 0, 0), memory_space=pltpu.SMEM),
                  pl.BlockSpec((1, tmc, d), lambda b, i: (b, i, 0)),
                  gspec,
                  pl.BlockSpec((tmc, 8), lambda b, i: (b * nl + i, 0)),
                  pl.BlockSpec((tmc, d), lambda b, i: (b * nl + i, 0)),
                  pl.BlockSpec(memory_space=pl.ANY)],
        out_specs=pl.BlockSpec((1, tmc, d), lambda b, i: (b, i, 0)),
        out_shape=jax.ShapeDtypeStruct((bx, l, d), F32),
        scratch_shapes=[pltpu.VMEM((TOP_K, tmc, d), F32), pltpu.SemaphoreType.DMA],
        compiler_params=_cparams(2),
        name="combine",
    )(dest, x, gate, w_tok, ysh, yb)


def _expert_block_rows(t_all, ne):
    avg = max(1, t_all * TOP_K // ne)
    if avg >= 512:
        return 1024, 256
    tm = max(64, 2 << avg.bit_length())
    return tm, tm // 4


def _moe(layer, xa, xb, mods_a, mods_b, norm_g, router_w, router_bias, w_gate, w_up, w_down, ws_gate, ws_up, ws_down):
    _, ne, d, _ = w_gate.shape
    ta, tb = xa.shape[0] * xa.shape[1], xb.shape[1]
    t_all = ta + tb
    tm, sub = _expert_block_rows(t_all, ne)
    n_blocks = (t_all * TOP_K + ne * (tm - 1)) // tm

    router_wt = router_w.T
    packed_a, lg_a = _norm_ffn(xa, norm_g, mods_a[0], mods_a[1], router_wt)
    packed_b, lg_b = _norm_ffn(xb, norm_g, mods_b[0], mods_b[1], router_wt)
    tra = _pick(xa.shape[1], (512, 256, 128, 64, 32, 16))
    eid_a, rank_a, w_a, cnt_a = _router(lg_a, router_bias, jnp.zeros((ne, LANE), F32), tra)
    eid_b, rank_b, w_b, cnt = _router(lg_b, router_bias, cnt_a, tb)

    counts = cnt[:, 0].astype(I32)
    nblk = (counts + tm - 1) // tm
    bend = jnp.cumsum(nblk)
    bstart = bend - nblk
    n_used = bend[-1]
    bidx = jnp.arange(n_blocks, dtype=I32)
    last = jnp.maximum(n_used - 1, 0)
    cb = jnp.minimum(bidx, last)
    block_e = jnp.minimum(jnp.sum((bend[None, :] <= cb[:, None]).astype(I32), axis=1), ne - 1)
    eids = jnp.arange(ne, dtype=I32)
    of_block = lambda v: jnp.sum(jnp.where(block_e[:, None] == eids, v, 0), axis=1)
    nrows = jnp.where(bidx < n_used, jnp.clip(of_block(counts) - (bidx - of_block(bstart)) * tm, 0, tm), 0)
    nrows = jnp.concatenate([nrows, n_used.reshape(1)]).astype(I32)
    pstart = (bstart * tm).astype(I32)
    zero_blk = jnp.where(counts > 0, pstart // sub + (counts - 1) // sub, -1).astype(I32)

    def sorted_rows(eid, rank):
        return rank + jnp.sum(jnp.where(eid[..., None] == eids, pstart, 0), axis=-1)

    dest_a, dest_b = sorted_rows(eid_a, rank_a), sorted_rows(eid_b, rank_b)
    xs_sorted = _dispatch(zero_blk, dest_a, packed_a, dest_b, packed_b, n_blocks * tm, sub)
    yb = _experts(layer, block_e, nrows, cb, xs_sorted, w_gate, w_up, w_down, tm=tm, sub=sub, n_blocks=n_blocks)

    outs = []
    for x, packed, mods, dest, w in ((xa, packed_a, mods_a, dest_a, w_a), (xb, packed_b, mods_b, dest_b, w_b)):
        nt = packed.shape[0]
        tms = _pick(nt, (1024, 512, 256, 128, 64, 32, 16))
        nsb = nt // tms
        ysh = _experts(layer, jnp.zeros((nsb,), I32), jnp.array([tms] * nsb + [nsb], I32), jnp.arange(nsb, dtype=I32),
                       packed, ws_gate[:, None], ws_up[:, None], ws_down[:, None],
                       tm=tms, sub=tms, n_blocks=nsb, full=True)
        outs.append(_combine(x, mods[2], dest, w.T, ysh, yb))
    return outs


def kernel(x_prompt, x_sample, state_lru_h, state_lru_conv, state_pool, c_prompt, c_sample, ada_w, ada_b, norm_mix, norm_ffn, norm_out, lru_w_in, lru_conv_w, lru_conv_b, lru_w_rg, lru_b_rg, lru_w_ig, lru_b_ig, lru_lambda, lru_w_out, pool_w, pool_scale, router_w, router_bias, exp_w_gate, exp_w_up, exp_w_down, shared_w_gate, shared_w_up, shared_w_down):
    bp, seq, d = x_prompt.shape
    bs = x_sample.shape[0]
    depth = ada_w.shape[0]
    ch = lru_w_in.shape[2] // 2
    assert x_sample.shape[1] == 1 and seq >= 16

    bc = -(-(bp + bs) // 8) * 8
    c_all = jnp.concatenate([c_prompt, c_sample, jnp.zeros((bc - bp - bs, d), F32)], axis=0)
    mod = _ada(c_all, ada_w, ada_b)

    def mods(i):
        mp = [mod[i, :bp, k * d:(k + 1) * d].reshape(bp, 1, d) for k in range(6)]
        ms = [mod[i, bp:bp + bs, k * d:(k + 1) * d].reshape(1, bs, d) for k in range(6)]
        return mp, ms

    xp = x_prompt
    xs = x_sample.reshape(1, bs, d)
    tn_in = _pick(ch, (896, 640, 512, 384, 256, 128))
    tn_out = _pick(d, (512, 256, 128))
    new_h_p, new_cv_p, new_pb_p, new_h_s, new_cv_s, new_pb_s = [], [], [], [], [], []
    for i in range(depth):
        mp, ms = mods(i)
        j = i // 2
        if i % 2 == 0:
            w_in = lru_w_in[j]
            lru = (lru_conv_w[j], lru_conv_b[j], lru_w_rg[j], lru_w_ig[j], lru_b_rg[j], lru_b_ig[j], lru_lambda[j])
            u = _norm_mix(xp, norm_mix[i], mp[0], mp[1])
            gate = _mm(u, w_in, col0=0, ncols=ch, tn=tn_in, act="gelu", name="lru_in_gate")
            rec = _mm(u, w_in, col0=ch, ncols=ch, tn=tn_in, name="lru_in_rec")
            ypre, h_last = _lru_seq(rec, gate, jnp.zeros((bp, ch), F32), jnp.zeros((bp, 3, ch), F32), *lru)
            xp = _mm(ypre, lru_w_out[j], tn=tn_out, resid=xp, gate=mp[2], name="lru_out")
            new_h_p.append(h_last.reshape(bp, ch))
            new_cv_p.append(rec[:, seq - 3:, :])
            u = _norm_mix(xs, norm_mix[i], ms[0], ms[1])
            gate = _mm(u, w_in, col0=0, ncols=ch, tn=tn_in, act="gelu", name="lru_in_gate")
            rec = _mm(u, w_in, col0=ch, ncols=ch, tn=tn_in, name="lru_in_rec")
            cv0 = state_lru_conv[j]
            ypre, h_new = _lru_step(rec[0], gate[0], state_lru_h[j], cv0.transpose(1, 0, 2), *lru)
            xs = _mm(ypre[None], lru_w_out[j], tn=tn_out, resid=xs, gate=ms[2], name="lru_out")
            new_h_s.append(h_new)
            new_cv_s.append(jnp.concatenate([cv0[:, 1:, :], rec[0][:, None, :]], axis=1))
        else:
            xp, pb = _pool_seq(xp, norm_mix[i], mp[0], mp[1], mp[2], jnp.zeros((bp, POOL_BUF, d), F32),
                               pool_w[j], pool_scale[j], 0)
            new_pb_p.append(pb)
            buf0 = state_pool[j]
            xs2, u_s = _pool_step(xs[0], norm_mix[i], ms[0][0], ms[1][0], ms[2][0], buf0.transpose(1, 0, 2),
                                  pool_w[j], pool_scale[j], PAST_LEN)
            xs = xs2[None]
            new_pb_s.append(jnp.concatenate([buf0[:, 1:, :], u_s[:, None, :]], axis=1))
        xp, xs = _moe(i, xp, xs, (mp[3], mp[4], mp[5]), (ms[3], ms[4], ms[5]), norm_ffn[i],
                      router_w[i], router_bias[i], exp_w_gate, exp_w_up, exp_w_down,
                      shared_w_gate, shared_w_up, shared_w_down)
    y_prompt = _norm_out(xp, norm_out)
    y_sample = _norm_out(xs, norm_out).reshape(bs, 1, d)
    return (y_prompt, y_sample, jnp.stack(new_h_p), jnp.stack(new_cv_p), jnp.stack(new_pb_p),
            jnp.stack(new_h_s), jnp.stack(new_cv_s), jnp.stack(new_pb_s))
```

```python
import functools

import jax
import jax.numpy as jnp
from jax import lax
from jax.experimental import pallas as pl
from jax.experimental.pallas import tpu as pltpu

F32 = jnp.float32
BF16 = jnp.bfloat16
I32 = jnp.int32
U32 = jnp.uint32

EPS = 1e-6
LRU_C = 8.0
LRU_BLOCK = 128
POOL_WINDOWS = (2, 4, 8, 16)
POOL_BUF = max(POOL_WINDOWS) - 1
TOP_K = 6
N_GROUPS = 8
TOPK_GROUPS = 4
ROUTED_SCALE = 2.5
PAST_LEN = 16384

V7X_VMEM_LIMIT_BYTES = 58 * 1024 * 1024
LANE = 128


def _cparams(n_axes):
    return pltpu.CompilerParams(dimension_semantics=("arbitrary",) * n_axes,
                                vmem_limit_bytes=V7X_VMEM_LIMIT_BYTES)


def _pick(n, candidates):
    for c in candidates:
        if c <= n and n % c == 0:
            return c
    return n


def _bdot(a, b):
    return jnp.dot(a.astype(BF16), b.astype(BF16), preferred_element_type=F32)


def _silu(x):
    return x * jax.nn.sigmoid(x)


def _ada_kernel(c_ref, w_ref, b_ref, o_ref):
    s = _silu(c_ref[...])
    o_ref[0] = _bdot(s, w_ref[0]) + b_ref[0]


def _ada(c_all, ada_w, ada_b):
    depth, d, n = ada_w.shape
    bc = c_all.shape[0]
    tn = _pick(n, (1024, 512, 256, 128))
    return pl.pallas_call(
        _ada_kernel,
        grid=(depth, n // tn),
        in_specs=[pl.BlockSpec((bc, d), lambda l, j: (0, 0)),
                  pl.BlockSpec((1, d, tn), lambda l, j: (l, 0, j)),
                  pl.BlockSpec((1, 1, tn), lambda l, j: (l, 0, j))],
        out_specs=pl.BlockSpec((1, bc, tn), lambda l, j: (l, 0, j)),
        out_shape=jax.ShapeDtypeStruct((depth, bc, n), F32),
        compiler_params=_cparams(2),
        name="ada_mod",
    )(c_all, ada_w, ada_b.reshape(depth, 1, n))


def _rms(x, g):
    return x * lax.rsqrt(jnp.mean(x * x, axis=-1, keepdims=True) + EPS) * g


def _modnorm(x, g, shift, scale):
    return _rms(x, g) * (1.0 + scale) + shift


def _mod_spec(mod, tm):
    d = mod.shape[-1]
    if mod.shape[1] == 1:
        return pl.BlockSpec((1, 1, d), lambda b, i: (b, 0, 0))
    return pl.BlockSpec((1, tm, d), lambda b, i: (b, i, 0))


def _norm_mix_kernel(x_ref, g_ref, sh_ref, sc_ref, o_ref):
    o_ref[0] = _modnorm(x_ref[0], g_ref[...], sh_ref[0], sc_ref[0]).astype(o_ref.dtype)


def _norm_mix(x, g, shift, scale):
    bx, l, d = x.shape
    tm = _pick(l, (512, 256, 128, 64, 32, 16))
    return pl.pallas_call(
        _norm_mix_kernel,
        grid=(bx, l // tm),
        in_specs=[pl.BlockSpec((1, tm, d), lambda b, i: (b, i, 0)),
                  pl.BlockSpec((1, d), lambda b, i: (0, 0)),
                  _mod_spec(shift, tm), _mod_spec(scale, tm)],
        out_specs=pl.BlockSpec((1, tm, d), lambda b, i: (b, i, 0)),
        out_shape=jax.ShapeDtypeStruct((bx, l, d), BF16),
        compiler_params=_cparams(2),
        name="norm_mix",
    )(x, g.reshape(1, d), shift, scale)


def _norm_out_kernel(x_ref, g_ref, o_ref):
    o_ref[0] = _rms(x_ref[0], g_ref[...])


def _norm_out(x, g):
    bx, l, d = x.shape
    tm = _pick(l, (512, 256, 128, 64, 32, 16, 8))
    return pl.pallas_call(
        _norm_out_kernel,
        grid=(bx, l // tm),
        in_specs=[pl.BlockSpec((1, tm, d), lambda b, i: (b, i, 0)),
                  pl.BlockSpec((1, d), lambda b, i: (0, 0))],
        out_specs=pl.BlockSpec((1, tm, d), lambda b, i: (b, i, 0)),
        out_shape=jax.ShapeDtypeStruct((bx, l, d), F32),
        compiler_params=_cparams(2),
        name="norm_out",
    )(x, g.reshape(1, d))


def _pack_pair(lo, hi):
    lo_b = lax.bitcast_convert_type(lo.astype(BF16).astype(F32), U32)
    hi_b = lax.bitcast_convert_type(hi.astype(BF16).astype(F32), U32)
    return lax.shift_right_logical(lo_b, jnp.uint32(16)) | (hi_b & jnp.uint32(0xFFFF0000))


def _unpack_pair(word):
    lo = lax.bitcast_convert_type(lax.shift_left(word, jnp.uint32(16)), F32)
    hi = lax.bitcast_convert_type(word & jnp.uint32(0xFFFF0000), F32)
    return lo, hi


def _norm_ffn_kernel(x_ref, g_ref, sh_ref, sc_ref, rwt_ref, up_ref, lg_ref):
    u = _modnorm(x_ref[0], g_ref[...], sh_ref[0], sc_ref[0])
    half = u.shape[1] // 2
    up_ref[...] = _pack_pair(u[:, :half], u[:, half:])
    lg_ref[...] = lax.dot_general(rwt_ref[...], u, (((1,), (1,)), ((), ())),
                                  precision=lax.Precision.HIGHEST, preferred_element_type=F32)


def _norm_ffn(x, g, shift, scale, router_wt):
    bx, l, d = x.shape
    e = router_wt.shape[0]
    tm = _pick(l, (512, 256, 128))
    nl = l // tm
    return pl.pallas_call(
        _norm_ffn_kernel,
        grid=(bx, nl),
        in_specs=[pl.BlockSpec((1, tm, d), lambda b, i: (b, i, 0)),
                  pl.BlockSpec((1, d), lambda b, i: (0, 0)),
                  _mod_spec(shift, tm), _mod_spec(scale, tm),
                  pl.BlockSpec((e, d), lambda b, i: (0, 0))],
        out_specs=[pl.BlockSpec((tm, d // 2), lambda b, i: (b * nl + i, 0)),
                   pl.BlockSpec((e, tm), lambda b, i: (0, b * nl + i))],
        out_shape=[jax.ShapeDtypeStruct((bx * l, d // 2), U32),
                   jax.ShapeDtypeStruct((e, bx * l), F32)],
        compiler_params=_cparams(2),
        name="norm_ffn",
    )(x, g.reshape(1, d), shift, scale, router_wt)


def _mm_kernel(*refs, act, resid):
    if resid:
        x_ref, w_ref, r_ref, g_ref, o_ref, wbf = refs
    else:
        x_ref, w_ref, o_ref, wbf = refs

    @pl.when((pl.program_id(1) == 0) & (pl.program_id(2) == 0))
    def _():
        wbf[...] = w_ref[...].astype(BF16)

    acc = jnp.dot(x_ref[0], wbf[...], preferred_element_type=F32)
    if act == "gelu":
        acc = jax.nn.gelu(acc)
    if resid:
        acc = r_ref[0] + g_ref[0] * acc
    o_ref[0] = acc.astype(o_ref.dtype)


def _mm(x, w, *, col0=0, ncols=None, tn, act=None, resid=None, gate=None, out_dtype=F32, name="mm"):
    bx, l, k = x.shape
    ncols = w.shape[1] - col0 if ncols is None else ncols
    tm = _pick(l, (1024, 512, 256, 128, 64, 32, 16))
    nj = ncols // tn
    j0 = col0 // tn
    in_specs = [pl.BlockSpec((1, tm, k), lambda j, b, i: (b, i, 0)),
                pl.BlockSpec((k, tn), lambda j, b, i: (0, j0 + j))]
    args = [x, w]
    if resid is not None:
        in_specs.append(pl.BlockSpec((1, tm, tn), lambda j, b, i: (b, i, j)))
        if gate.shape[1] == 1:
            in_specs.append(pl.BlockSpec((1, 1, tn), lambda j, b, i: (b, 0, j)))
        else:
            in_specs.append(pl.BlockSpec((1, tm, tn), lambda j, b, i: (b, i, j)))
        args += [resid, gate]
    return pl.pallas_call(
        functools.partial(_mm_kernel, act=act, resid=resid is not None),
        grid=(nj, bx, l // tm),
        in_specs=in_specs,
        out_specs=pl.BlockSpec((1, tm, tn), lambda j, b, i: (b, i, j)),
        out_shape=jax.ShapeDtypeStruct((bx, l, ncols), out_dtype),
        scratch_shapes=[pltpu.VMEM((k, tn), BF16)],
        compiler_params=_cparams(3),
        name=name,
    )(*args)


def _lru_coeffs(xc, wrg, wig, brg, big, lam):
    heads = xc.shape[1] // LRU_BLOCK
    rs, igs = [], []
    for h in range(heads):
        xh = xc[:, h * LRU_BLOCK:(h + 1) * LRU_BLOCK].astype(BF16)
        rs.append(jnp.dot(xh, wrg[h], preferred_element_type=F32))
        igs.append(jnp.dot(xh, wig[h], preferred_element_type=F32))
    r = jax.nn.sigmoid(jnp.concatenate(rs, axis=1) + brg)
    ig = jax.nn.sigmoid(jnp.concatenate(igs, axis=1) + big)
    log_a = -LRU_C * r * jax.nn.softplus(-lam)
    a = jnp.exp(log_a)
    b = jnp.sqrt(1.0 - jnp.exp(2.0 * log_a)) * (ig * xc)
    return a, b


def _lru_seq_kernel(rec_ref, gate_ref, h0_ref, cv0_ref, cw_ref, cb_ref, wrg_ref, wig_ref, brg_ref,
                    big_ref, lam_ref, y_ref, hout_ref, ebuf, hcar, wrgb, wigb, *, tc):
    c = pl.program_id(2)

    @pl.when(c == 0)
    def _():
        ebuf[5:8, :] = cv0_ref[0]
        hcar[...] = h0_ref[0]
        wrgb[...] = wrg_ref[...].astype(BF16)
        wigb[...] = wig_ref[...].astype(BF16)

    rec = rec_ref[0]
    ebuf[8:8 + tc, :] = rec
    cw = cw_ref[...]
    xc = cb_ref[...] + ebuf[5:5 + tc, :] * cw[0:1] + ebuf[6:6 + tc, :] * cw[1:2] \
        + ebuf[7:7 + tc, :] * cw[2:3] + rec * cw[3:4]
    a, b = _lru_coeffs(xc, wrgb, wigb, brg_ref[...], big_ref[...], lam_ref[...])

    ct = a.shape[1]
    ng = tc // 8
    a3, b3 = a.reshape(ng, 8, ct), b.reshape(ng, 8, ct)
    row = lax.broadcasted_iota(I32, a3.shape, 1)
    for s in (1, 2, 4):
        m = row >= s
        a_sh = pltpu.roll(a3, s, 1)
        b_sh = pltpu.roll(b3, s, 1)
        b3 = jnp.where(m, a3 * b_sh + b3, b3)
        a3 = jnp.where(m, a3 * a_sh, a3)
    carry = hcar[...]
    hs = []
    for g in range(ng):
        hg = a3[g] * carry + b3[g]
        hs.append(hg)
        carry = hg[7:8, :]
    h = jnp.concatenate(hs, axis=0)
    y_ref[0] = (gate_ref[0] * h).astype(y_ref.dtype)
    hlast = h[tc - 1:tc, :]
    hcar[...] = hlast
    hout_ref[0] = hlast
    ebuf[5:8, :] = rec[tc - 3:tc, :]


def _lru_seq(rec, gate, h0, cv0, conv_w, conv_b, w_rg, w_ig, b_rg, b_ig, lam):
    bx, l, ch = rec.shape
    tc = _pick(l, (256, 128, 64, 32, 16, 8))
    heads = ch // LRU_BLOCK
    hpt = 7 if heads % 7 == 0 else (5 if heads % 5 == 0 else 1)
    ct = hpt * LRU_BLOCK
    nct = ch // ct
    row = lambda a: a.reshape(1, ch)
    cspec = lambda r: pl.BlockSpec((r, ct), lambda b, j, c: (0, j))
    return pl.pallas_call(
        functools.partial(_lru_seq_kernel, tc=tc),
        grid=(bx, nct, l // tc),
        in_specs=[pl.BlockSpec((1, tc, ct), lambda b, j, c: (b, c, j)),
                  pl.BlockSpec((1, tc, ct), lambda b, j, c: (b, c, j)),
                  pl.BlockSpec((1, 1, ct), lambda b, j, c: (b, 0, j)),
                  pl.BlockSpec((1, 3, ct), lambda b, j, c: (b, 0, j)),
                  cspec(4), cspec(1),
                  pl.BlockSpec((hpt, LRU_BLOCK, LRU_BLOCK), lambda b, j, c: (j, 0, 0)),
                  pl.BlockSpec((hpt, LRU_BLOCK, LRU_BLOCK), lambda b, j, c: (j, 0, 0)),
                  cspec(1), cspec(1), cspec(1)],
        out_specs=[pl.BlockSpec((1, tc, ct), lambda b, j, c: (b, c, j)),
                   pl.BlockSpec((1, 1, ct), lambda b, j, c: (b, 0, j))],
        out_shape=[jax.ShapeDtypeStruct((bx, l, ch), BF16),
                   jax.ShapeDtypeStruct((bx, 1, ch), F32)],
        scratch_shapes=[pltpu.VMEM((8 + tc, ct), F32), pltpu.VMEM((1, ct), F32),
                        pltpu.VMEM((hpt, LRU_BLOCK, LRU_BLOCK), BF16),
                        pltpu.VMEM((hpt, LRU_BLOCK, LRU_BLOCK), BF16)],
        compiler_params=_cparams(3),
        name="lru_seq",
    )(rec, gate, h0.reshape(bx, 1, ch), cv0, conv_w, row(conv_b), w_rg, w_ig, row(b_rg), row(b_ig), row(lam))


def _lru_step_kernel(rec_ref, gate_ref, h0_ref, cv0_ref, cw_ref, cb_ref, wrg_ref, wig_ref, brg_ref,
                     big_ref, lam_ref, y_ref, hout_ref):
    rec = rec_ref[...]
    cw = cw_ref[...]
    xc = cb_ref[...] + cv0_ref[0] * cw[0:1] + cv0_ref[1] * cw[1:2] + cv0_ref[2] * cw[2:3] + rec * cw[3:4]
    a, b = _lru_coeffs(xc, wrg_ref[...].astype(BF16), wig_ref[...].astype(BF16),
                       brg_ref[...], big_ref[...], lam_ref[...])
    h = a * h0_ref[...] + b
    y_ref[...] = (gate_ref[...] * h).astype(y_ref.dtype)
    hout_ref[...] = h


def _lru_step(rec, gate, h0, cv0_t, conv_w, conv_b, w_rg, w_ig, b_rg, b_ig, lam):
    bx, ch = rec.shape
    heads = ch // LRU_BLOCK
    hpt = 7 if heads % 7 == 0 else (5 if heads % 5 == 0 else 1)
    ct = hpt * LRU_BLOCK
    row = lambda a: a.reshape(1, ch)
    bspec = pl.BlockSpec((bx, ct), lambda j: (0, j))
    cspec = lambda r: pl.BlockSpec((r, ct), lambda j: (0, j))
    wspec = pl.BlockSpec((hpt, LRU_BLOCK, LRU_BLOCK), lambda j: (j, 0, 0))
    return pl.pallas_call(
        _lru_step_kernel,
        grid=(ch // ct,),
        in_specs=[bspec, bspec, bspec, pl.BlockSpec((3, bx, ct), lambda j: (0, 0, j)),
                  cspec(4), cspec(1), wspec, wspec, cspec(1), cspec(1), cspec(1)],
        out_specs=[bspec, bspec],
        out_shape=[jax.ShapeDtypeStruct((bx, ch), BF16), jax.ShapeDtypeStruct((bx, ch), F32)],
        compiler_params=_cparams(1),
        name="lru_step",
    )(rec, gate, h0, cv0_t, conv_w, row(conv_b), w_rg, w_ig, row(b_rg), row(b_ig), row(lam))


def _pool_seq_kernel(x_ref, g_ref, sh_ref, sc_ref, gt_ref, buf0_ref, pw_ref, ps_ref, o_ref, st_ref,
                     ebuf, pwb, *, tc, pos0):
    c = pl.program_id(1)
    d = x_ref.shape[2]
    gd = d // len(POOL_WINDOWS)

    @pl.when(c == 0)
    def _():
        ebuf[1:16, :] = buf0_ref[0]
        pwb[...] = pw_ref[...].astype(BF16)

    x = x_ref[0]
    u = _modnorm(x, g_ref[...], sh_ref[0], sc_ref[0])
    ebuf[16:16 + tc, :] = u
    pos = pos0 + c * tc + lax.broadcasted_iota(I32, (tc, gd), 0)
    ys = []
    for gi, w in enumerate(POOL_WINDOWS):
        sl = slice(gi * gd, (gi + 1) * gd)
        ssum = u[:, sl]
        for j in range(1, w):
            ssum = ssum + ebuf[16 - j:16 - j + tc, sl]
        cnt = jnp.minimum(w, pos + 1).astype(F32)
        p = ssum / cnt - u[:, sl]
        ys.append(jnp.dot(p.astype(BF16), pwb[gi], preferred_element_type=F32))
    y = jnp.concatenate(ys, axis=1) * ps_ref[...]
    o_ref[0] = x + gt_ref[0] * y
    st_ref[0] = ebuf[tc + 1:tc + 16, :]
    ebuf[0:16, :] = ebuf[tc:tc + 16, :]


def _pool_seq(x, g, shift, scale, gate, buf0, pool_w, pool_scale, pos0):
    bx, l, d = x.shape
    tc = _pick(l, (256, 128, 64, 32, 16))
    ng, gd, _ = pool_w.shape
    mspec = lambda m: (pl.BlockSpec((1, 1, d), lambda b, c: (b, 0, 0)) if m.shape[1] == 1
                       else pl.BlockSpec((1, tc, d), lambda b, c: (b, c, 0)))
    return pl.pallas_call(
        functools.partial(_pool_seq_kernel, tc=tc, pos0=pos0),
        grid=(bx, l // tc),
        in_specs=[pl.BlockSpec((1, tc, d), lambda b, c: (b, c, 0)),
                  pl.BlockSpec((1, d), lambda b, c: (0, 0)),
                  mspec(shift), mspec(scale), mspec(gate),
                  pl.BlockSpec((1, POOL_BUF, d), lambda b, c: (b, 0, 0)),
                  pl.BlockSpec((ng, gd, gd), lambda b, c: (0, 0, 0)),
                  pl.BlockSpec((1, d), lambda b, c: (0, 0))],
        out_specs=[pl.BlockSpec((1, tc, d), lambda b, c: (b, c, 0)),
                   pl.BlockSpec((1, POOL_BUF, d), lambda b, c: (b, 0, 0))],
        out_shape=[jax.ShapeDtypeStruct((bx, l, d), F32),
                   jax.ShapeDtypeStruct((bx, POOL_BUF, d), F32)],
        scratch_shapes=[pltpu.VMEM((16 + tc, d), F32), pltpu.VMEM((ng, gd, gd), BF16)],
        compiler_params=_cparams(2),
        name="pool_seq",
    )(x, g.reshape(1, d), shift, scale, gate, buf0, pool_w, pool_scale.reshape(1, d))


def _pool_step_kernel(x_ref, g_ref, sh_ref, sc_ref, gt_ref, buf_ref, pw_ref, ps_ref, o_ref, u_ref, *, pos0):
    d = x_ref.shape[1]
    gd = d // len(POOL_WINDOWS)
    x = x_ref[...]
    u = _modnorm(x, g_ref[...], sh_ref[...], sc_ref[...])
    ys = []
    for gi, w in enumerate(POOL_WINDOWS):
        sl = slice(gi * gd, (gi + 1) * gd)
        ssum = u[:, sl]
        for j in range(1, w):
            ssum = ssum + buf_ref[POOL_BUF - j, :, sl]
        cnt = float(min(w, pos0 + 1))
        p = ssum / cnt - u[:, sl]
        ys.append(_bdot(p, pw_ref[gi]))
    y = jnp.concatenate(ys, axis=1) * ps_ref[...]
    o_ref[...] = x + gt_ref[...] * y
    u_ref[...] = u


def _pool_step(x, g, shift, scale, gate, buf_t, pool_w, pool_scale, pos0):
    bx, d = x.shape
    ng, gd, _ = pool_w.shape
    full = pl.BlockSpec((bx, d), lambda i: (0, 0))
    vec = pl.BlockSpec((1, d), lambda i: (0, 0))
    return pl.pallas_call(
        functools.partial(_pool_step_kernel, pos0=pos0),
        grid=(1,),
        in_specs=[full, vec, full, full, full,
                  pl.BlockSpec((POOL_BUF, bx, d), lambda i: (0, 0, 0)),
                  pl.BlockSpec((ng, gd, gd), lambda i: (0, 0, 0)), vec],
        out_specs=[full, full],
        out_shape=[jax.ShapeDtypeStruct((bx, d), F32), jax.ShapeDtypeStruct((bx, d), F32)],
        compiler_params=_cparams(1),
        name="pool_step",
    )(x, g.reshape(1, d), shift, scale, gate, buf_t, pool_w, pool_scale.reshape(1, d))


def _first_max(v, idx, n):
    m = jnp.max(v, axis=0, keepdims=True)
    first = jnp.min(jnp.where(v == m, idx, n), axis=0, keepdims=True)
    return jnp.where(idx == first, 1.0, 0.0), m, first


def _stack_rows(rows, dtype):
    n = rows[0].shape[1]
    r8 = lax.broadcasted_iota(I32, (8, n), 0)
    out = jnp.zeros((8, n), dtype)
    for k, r in enumerate(rows):
        out = jnp.where(r8 == k, jnp.broadcast_to(r.astype(dtype), (8, n)), out)
    return out


def _router_kernel(lg_ref, bias_ref, cin_ref, eid_ref, rank_ref, w_ref, cnt_ref, carry, *, tr):
    i = pl.program_id(0)

    @pl.when(i == 0)
    def _():
        carry[...] = cin_ref[:, 0:1]

    ne = lg_ref.shape[0]
    gs = ne // N_GROUPS
    s = jax.nn.sigmoid(lg_ref[...])
    biased = s + bias_ref[...]
    neg = -jnp.inf
    idx8 = lax.broadcasted_iota(I32, (gs, tr), 0)
    gidx = lax.broadcasted_iota(I32, (N_GROUPS, tr), 0)
    gsc = jnp.zeros((N_GROUPS, tr), F32)
    for g in range(N_GROUPS):
        blk = biased[g * gs:(g + 1) * gs, :]
        oh, m1, _ = _first_max(blk, idx8, gs)
        m2 = jnp.max(jnp.where(oh > 0, neg, blk), axis=0, keepdims=True)
        gsc = jnp.where(gidx == g, jnp.broadcast_to(m1 + m2, (N_GROUPS, tr)), gsc)
    gsel = jnp.zeros((N_GROUPS, tr), F32)
    for _ in range(TOPK_GROUPS):
        oh, _, _ = _first_max(gsc, gidx, N_GROUPS)
        gsel = gsel + oh
        gsc = jnp.where(oh > 0, neg, gsc)
    emask = jnp.concatenate([jnp.broadcast_to(gsel[g:g + 1, :], (gs, tr)) for g in range(N_GROUPS)], axis=0)
    masked = jnp.where(emask > 0, biased, neg)
    eidx = lax.broadcasted_iota(I32, (ne, tr), 0)
    ohs, ws, ids = [], [], []
    for _ in range(TOP_K):
        oh, _, first = _first_max(masked, eidx, ne)
        ohs.append(oh)
        ids.append(first)
        ws.append(jnp.sum(oh * s, axis=0, keepdims=True))
        masked = jnp.where(oh > 0, neg, masked)
    wsum = ws[0]
    cm = ohs[0]
    for k in range(1, TOP_K):
        wsum = wsum + ws[k]
        cm = cm + ohs[k]
    tri = (lax.broadcasted_iota(I32, (tr, tr), 0) < lax.broadcasted_iota(I32, (tr, tr), 1))
    prefix = jnp.dot(cm.astype(BF16), jnp.where(tri, 1.0, 0.0).astype(BF16),
                     preferred_element_type=F32) + carry[...]
    rk = [jnp.sum(ohs[k] * prefix, axis=0, keepdims=True) for k in range(TOP_K)]
    rank_ref[0] = _stack_rows(rk, F32).astype(I32)
    eid_ref[0] = _stack_rows(ids, I32)
    w_ref[...] = _stack_rows([w / wsum * ROUTED_SCALE for w in ws], F32)
    carry[...] = carry[...] + jnp.sum(cm, axis=1, keepdims=True)
    cnt_ref[...] = jnp.broadcast_to(carry[...], cnt_ref.shape)


def _router(logits_t, bias, counts_in, tr):
    ne, t = logits_t.shape
    nt = t // tr
    tile = pl.BlockSpec((1, 8, tr), lambda i: (i, 0, 0))
    return pl.pallas_call(
        functools.partial(_router_kernel, tr=tr),
        grid=(nt,),
        in_specs=[pl.BlockSpec((ne, tr), lambda i: (0, i)), pl.BlockSpec((ne, 1), lambda i: (0, 0)),
                  pl.BlockSpec((ne, LANE), lambda i: (0, 0))],
        out_specs=[tile, tile, pl.BlockSpec((8, tr), lambda i: (0, i)),
                   pl.BlockSpec((ne, LANE), lambda i: (0, 0))],
        out_shape=[jax.ShapeDtypeStruct((nt, 8, tr), I32), jax.ShapeDtypeStruct((nt, 8, tr), I32),
                   jax.ShapeDtypeStruct((8, t), F32), jax.ShapeDtypeStruct((ne, LANE), F32)],
        scratch_shapes=[pltpu.VMEM((ne, 1), F32)],
        compiler_params=_cparams(1),
        name="router",
    )(logits_t, bias.reshape(ne, 1), counts_in)


def _dispatch_kernel(zb_ref, desta_ref, destb_ref, xa_ref, xb_ref, xs_ref, zbuf, sem, zsem, *, nta, ne):
    i = pl.program_id(0)
    zrows = zbuf.shape[0]

    @pl.when(i == 0)
    def _():
        zbuf[...] = jnp.zeros_like(zbuf)

        def zero_copy(e):
            return pltpu.make_async_copy(zbuf, xs_ref.at[pl.ds(zb_ref[e] * zrows, zrows), :], zsem)

        def zstart(e, c):
            @pl.when(zb_ref[e] >= 0)
            def _():
                zero_copy(e).start()
            return c

        def zwait(e, c):
            @pl.when(zb_ref[e] >= 0)
            def _():
                zero_copy(e).wait()
            return c

        lax.fori_loop(0, ne, zstart, 0)
        lax.fori_loop(0, ne, zwait, 0)

    def scatter(x_ref, dest_ref):
        def start(t, c):
            for k in range(TOP_K):
                pltpu.make_async_copy(x_ref.at[pl.ds(t, 1), :], xs_ref.at[pl.ds(dest_ref[0, k, t], 1), :],
                                      sem).start(priority=k % 2)
            return c

        def wait(t, c):
            for k in range(TOP_K):
                pltpu.make_async_copy(x_ref.at[pl.ds(0, 1), :], xs_ref.at[pl.ds(0, 1), :], sem).wait()
            return c

        lax.fori_loop(0, x_ref.shape[0], start, 0)
        lax.fori_loop(0, x_ref.shape[0], wait, 0)

    @pl.when(i < nta)
    def _():
        scatter(xa_ref, desta_ref)

    @pl.when(i == nta)
    def _():
        scatter(xb_ref, destb_ref)


def _dispatch(zero_blk, dest_a, packed_a, dest_b, packed_b, r_max, zrows):
    nta, _, tda = dest_a.shape
    _, _, tdb = dest_b.shape
    width = packed_a.shape[1]
    ne = zero_blk.shape[0]
    assert packed_a.shape[0] == nta * tda and packed_b.shape[0] == tdb and dest_b.shape[0] == 1
    return pl.pallas_call(
        functools.partial(_dispatch_kernel, nta=nta, ne=ne),
        grid_spec=pltpu.PrefetchScalarGridSpec(
            num_scalar_prefetch=1,
            grid=(nta + 1,),
            in_specs=[pl.BlockSpec((1, 8, tda), lambda i, zb: (jnp.minimum(i, nta - 1), 0, 0),
                                   memory_space=pltpu.SMEM),
                      pl.BlockSpec((1, 8, tdb), lambda i, zb: (0, 0, 0), memory_space=pltpu.SMEM),
                      pl.BlockSpec((tda, width), lambda i, zb: (jnp.minimum(i, nta - 1), 0)),
                      pl.BlockSpec((tdb, width), lambda i, zb: (0, 0))],
            out_specs=pl.BlockSpec(memory_space=pl.ANY),
            scratch_shapes=[pltpu.VMEM((zrows, width), U32), pltpu.SemaphoreType.DMA, pltpu.SemaphoreType.DMA]),
        out_shape=jax.ShapeDtypeStruct((r_max, width), U32),
        compiler_params=_cparams(1),
        name="dispatch",
    )(zero_blk, dest_a, dest_b, packed_a, packed_b)


EXPERT_F_CHUNK = 256
EXPERT_RING = 3


def _expert_kernel(be_ref, nr_ref, xb_ref, x_ref, wg_hbm, wu_hbm, wd_hbm, o_ref, xbf, gu_ring, wd_ring, sem,
                   *, layer, widths, sub, full):
    del xb_ref
    b = pl.program_id(0)
    n_used = nr_ref[pl.num_programs(0)]
    n = nr_ref[b]
    nsub = x_ref.shape[0] // sub
    nc = len(widths)
    starts = [sum(widths[:c]) for c in range(nc)]

    def copies(blk, c):
        e = be_ref[blk]
        slot, c0, w = c % EXPERT_RING, starts[c], widths[c]
        return (
            pltpu.make_async_copy(wg_hbm.at[layer, e, :, pl.ds(c0, w)], gu_ring.at[slot, :, pl.ds(0, w)], sem.at[slot]),
            pltpu.make_async_copy(wu_hbm.at[layer, e, :, pl.ds(c0, w)], gu_ring.at[slot, :, pl.ds(w, w)], sem.at[slot]),
            pltpu.make_async_copy(wd_hbm.at[layer, e, pl.ds(c0, w), :], wd_ring.at[slot, pl.ds(0, w), :], sem.at[slot]))

    def start(blk, c):
        for cp in copies(blk, c):
            cp.start()

    @pl.when(b < n_used)
    def _():
        @pl.when(b == 0)
        def _():
            start(0, 0)
            start(0, 1)

        def unpack(rows):
            lo, hi = _unpack_pair(x_ref[rows, :])
            xbf[rows, :] = jnp.concatenate([lo, hi], axis=1).astype(BF16)

        for s in range(nsub):
            rows = slice(s * sub, (s + 1) * sub)
            if full:
                unpack(rows)
            else:
                pl.when(s * sub < n)(functools.partial(unpack, rows))

        for c in range(nc):
            nxt, ahead = (c + 2) % nc, (c + 2) // nc
            if ahead == 0:
                start(b, nxt)
            else:
                @pl.when(b + ahead < n_used)
                def _():
                    start(b + ahead, nxt)
            for cp in copies(b, c):
                cp.wait()
            slot, w = c % EXPERT_RING, widths[c]

            def part(r0, nr):
                rows = slice(r0, r0 + nr)
                gu = jnp.dot(xbf[rows, :], gu_ring[slot, :, 0:2 * w].astype(BF16), preferred_element_type=F32)
                h = (_silu(gu[:, :w]) * gu[:, w:]).astype(BF16)
                y = jnp.dot(h, wd_ring[slot, 0:w, :].astype(BF16), preferred_element_type=F32)
                if c == 0:
                    o_ref[rows, :] = y
                else:
                    o_ref[rows, :] += y

            if full:
                part(0, nsub * sub)
            else:
                for r0 in (0, 2 * sub):
                    pl.when(n > r0 + sub)(functools.partial(part, r0, 2 * sub))
                    pl.when((n > r0) & (n <= r0 + sub))(functools.partial(part, r0, sub))


def _experts(layer, block_e, nrows, xblk, xs, w_gate, w_up, w_down, *, tm, sub, n_blocks, full=False):
    _, ne, d, f = w_gate.shape
    width = xs.shape[1]
    tf = min(EXPERT_F_CHUNK, f)
    widths = (tf,) * (f // tf) + ((f % tf,) if f % tf else ())
    assert len(widths) >= EXPERT_RING and len(widths) % EXPERT_RING == 0, widths
    assert (full or tm == 4 * sub) and tm % sub == 0 and nrows.shape[0] == n_blocks + 1
    return pl.pallas_call(
        functools.partial(_expert_kernel, layer=layer, widths=widths, sub=sub, full=full),
        grid_spec=pltpu.PrefetchScalarGridSpec(
            num_scalar_prefetch=3,
            grid=(n_blocks,),
            in_specs=[pl.BlockSpec((tm, width), lambda b, be, nu, xb: (xb[b], 0)),
                      pl.BlockSpec(memory_space=pl.ANY), pl.BlockSpec(memory_space=pl.ANY),
                      pl.BlockSpec(memory_space=pl.ANY)],
            out_specs=pl.BlockSpec((tm, d), lambda b, be, nu, xb: (xb[b], 0)),
            scratch_shapes=[pltpu.VMEM((tm, d), BF16),
                            pltpu.VMEM((EXPERT_RING, d, 2 * tf), F32),
                            pltpu.VMEM((EXPERT_RING, tf, d), F32),
                            pltpu.SemaphoreType.DMA((EXPERT_RING,))]),
        out_shape=jax.ShapeDtypeStruct((n_blocks * tm, d), F32),
        compiler_params=_cparams(1),
        name="experts",
    )(block_e, nrows, xblk, xs, w_gate, w_up, w_down)


def _combine_kernel(dest_ref, x_ref, gt_ref, w_ref, ysh_ref, yb_ref, o_ref, gbuf, sem, *, tmc, per_tile):
    off = (pl.program_id(1) % per_tile) * tmc

    def start(t, c):
        for k in range(TOP_K):
            pltpu.make_async_copy(yb_ref.at[pl.ds(dest_ref[0, k, off + t], 1), :], gbuf.at[k, pl.ds(t, 1), :],
                                  sem).start(priority=k % 2)
        return c

    def wait(t, c):
        for k in range(TOP_K):
            pltpu.make_async_copy(yb_ref.at[pl.ds(0, 1), :], gbuf.at[0, pl.ds(0, 1), :], sem).wait()
        return c

    lax.fori_loop(0, tmc, start, 0)
    lax.fori_loop(0, tmc, wait, 0)
    w = w_ref[...]
    acc = w[:, 0:1] * gbuf[0]
    for k in range(1, TOP_K):
        acc = acc + w[:, k:k + 1] * gbuf[k]
    o_ref[0] = x_ref[0] + gt_ref[0] * (acc + ysh_ref[...])


def _combine(x, gate, dest, w_tok, ysh, yb):
    bx, l, d = x.shape
    tr = dest.shape[2]
    tmc = _pick(tr, (256, 128, 64, 32, 16))
    per_tile = tr // tmc
    nl = l // tmc
    assert l % tr == 0
    gspec = (pl.BlockSpec((1, 1, d), lambda b, i: (b, 0, 0)) if gate.shape[1] == 1
             else pl.BlockSpec((1, tmc, d), lambda b, i: (b, i, 0)))
    return pl.pallas_call(
        functools.partial(_combine_kernel, tmc=tmc, per_tile=per_tile),
        grid=(bx, nl),
        in_specs=[pl.BlockSpec((1, 8, tr), lambda b, i: ((b * nl + i) // per_tile, 0, 0), memory_space=pltpu.SMEM),
                  pl.BlockSpec((1, tmc, d), lambda b, i: (b, i, 0)),
                  gspec,
                  pl.BlockSpec((tmc, 8), lambda b, i: (b * nl + i, 0)),
                  pl.BlockSpec((tmc, d), lambda b, i: (b * nl + i, 0)),
                  pl.BlockSpec(memory_space=pl.ANY)],
        out_specs=pl.BlockSpec((1, tmc, d), lambda b, i: (b, i, 0)),
        out_shape=jax.ShapeDtypeStruct((bx, l, d), F32),
        scratch_shapes=[pltpu.VMEM((TOP_K, tmc, d), F32), pltpu.SemaphoreType.DMA],
        compiler_params=_cparams(2),
        name="combine",
    )(dest, x, gate, w_tok, ysh, yb)


def _expert_block_rows(t_all, ne):
    avg = max(1, t_all * TOP_K // ne)
    if avg >= 512:
        return 1024, 256
    tm = max(64, 2 << avg.bit_length())
    return tm, tm // 4


def _moe(layer, xa, xb, mods_a, mods_b, norm_g, router_w, router_bias, w_gate, w_up, w_down, ws_gate, ws_up, ws_down):
    _, ne, d, _ = w_gate.shape
    ta, tb = xa.shape[0] * xa.shape[1], xb.shape[1]
    t_all = ta + tb
    tm, sub = _expert_block_rows(t_all, ne)
    n_blocks = (t_all * TOP_K + ne * (tm - 1)) // tm

    router_wt = router_w.T
    packed_a, lg_a = _norm_ffn(xa, norm_g, mods_a[0], mods_a[1], router_wt)
    packed_b, lg_b = _norm_ffn(xb, norm_g, mods_b[0], mods_b[1], router_wt)
    tra = _pick(xa.shape[1], (1024, 512, 256, 128, 64, 32, 16))
    eid_a, rank_a, w_a, cnt_a = _router(lg_a, router_bias, jnp.zeros((ne, LANE), F32), tra)
    eid_b, rank_b, w_b, cnt = _router(lg_b, router_bias, cnt_a, tb)

    counts = cnt[:, 0].astype(I32)
    nblk = (counts + tm - 1) // tm
    bend = jnp.cumsum(nblk)
    bstart = bend - nblk
    n_used = bend[-1]
    bidx = jnp.arange(n_blocks, dtype=I32)
    last = jnp.maximum(n_used - 1, 0)
    cb = jnp.minimum(bidx, last)
    block_e = jnp.minimum(jnp.sum((bend[None, :] <= cb[:, None]).astype(I32), axis=1), ne - 1)
    eids = jnp.arange(ne, dtype=I32)
    of_block = lambda v: jnp.sum(jnp.where(block_e[:, None] == eids, v, 0), axis=1)
    nrows = jnp.where(bidx < n_used, jnp.clip(of_block(counts) - (bidx - of_block(bstart)) * tm, 0, tm), 0)
    nrows = jnp.concatenate([nrows, n_used.reshape(1)]).astype(I32)
    pstart = (bstart * tm).astype(I32)
    zero_blk = jnp.where(counts > 0, pstart // sub + (counts - 1) // sub, -1).astype(I32)

    def sorted_rows(eid, rank):
        return rank + jnp.sum(jnp.where(eid[..., None] == eids, pstart, 0), axis=-1)

    dest_a, dest_b = sorted_rows(eid_a, rank_a), sorted_rows(eid_b, rank_b)
    xs_sorted = _dispatch(zero_blk, dest_a, packed_a, dest_b, packed_b, n_blocks * tm, sub)
    yb = _experts(layer, block_e, nrows, cb, xs_sorted, w_gate, w_up, w_down, tm=tm, sub=sub, n_blocks=n_blocks)

    outs = []
    for x, packed, mods, dest, w in ((xa, packed_a, mods_a, dest_a, w_a), (xb, packed_b, mods_b, dest_b, w_b)):
        nt = packed.shape[0]
        tms = _pick(nt, (1024, 512, 256, 128, 64, 32, 16))
        nsb = nt // tms
        ysh = _experts(layer, jnp.zeros((nsb,), I32), jnp.array([tms] * nsb + [nsb], I32), jnp.arange(nsb, dtype=I32),
                       packed, ws_gate[:, None], ws_up[:, None], ws_down[:, None],
                       tm=tms, sub=tms, n_blocks=nsb, full=True)
        outs.append(_combine(x, mods[2], dest, w.T, ysh, yb))
    return outs


def kernel(x_prompt, x_sample, state_lru_h, state_lru_conv, state_pool, c_prompt, c_sample, ada_w, ada_b, norm_mix, norm_ffn, norm_out, lru_w_in, lru_conv_w, lru_conv_b, lru_w_rg, lru_b_rg, lru_w_ig, lru_b_ig, lru_lambda, lru_w_out, pool_w, pool_scale, router_w, router_bias, exp_w_gate, exp_w_up, exp_w_down, shared_w_gate, shared_w_up, shared_w_down):
    bp, seq, d = x_prompt.shape
    bs = x_sample.shape[0]
    depth = ada_w.shape[0]
    ch = lru_w_in.shape[2] // 2
    assert x_sample.shape[1] == 1 and seq >= 16

    bc = -(-(bp + bs) // 8) * 8
    c_all = jnp.concatenate([c_prompt, c_sample, jnp.zeros((bc - bp - bs, d), F32)], axis=0)
    mod = _ada(c_all, ada_w, ada_b)

    def mods(i):
        mp = [mod[i, :bp, k * d:(k + 1) * d].reshape(bp, 1, d) for k in range(6)]
        ms = [mod[i, bp:bp + bs, k * d:(k + 1) * d].reshape(1, bs, d) for k in range(6)]
        return mp, ms

    xp = x_prompt
    xs = x_sample.reshape(1, bs, d)
    tn_in = _pick(ch, (896, 640, 512, 384, 256, 128))
    tn_out = _pick(d, (512, 256, 128))
    new_h_p, new_cv_p, new_pb_p, new_h_s, new_cv_s, new_pb_s = [], [], [], [], [], []
    for i in range(depth):
        mp, ms = mods(i)
        j = i // 2
        if i % 2 == 0:
            w_in = lru_w_in[j]
            lru = (lru_conv_w[j], lru_conv_b[j], lru_w_rg[j], lru_w_ig[j], lru_b_rg[j], lru_b_ig[j], lru_lambda[j])
            u = _norm_mix(xp, norm_mix[i], mp[0], mp[1])
            gate = _mm(u, w_in, col0=0, ncols=ch, tn=tn_in, act="gelu", name="lru_in_gate")
            rec = _mm(u, w_in, col0=ch, ncols=ch, tn=tn_in, name="lru_in_rec")
            ypre, h_last = _lru_seq(rec, gate, jnp.zeros((bp, ch), F32), jnp.zeros((bp, 3, ch), F32), *lru)
            xp = _mm(ypre, lru_w_out[j], tn=tn_out, resid=xp, gate=mp[2], name="lru_out")
            new_h_p.append(h_last.reshape(bp, ch))
            new_cv_p.append(rec[:, seq - 3:, :])
            u = _norm_mix(xs, norm_mix[i], ms[0], ms[1])
            gate = _mm(u, w_in, col0=0, ncols=ch, tn=tn_in, act="gelu", name="lru_in_gate")
            rec = _mm(u, w_in, col0=ch, ncols=ch, tn=tn_in, name="lru_in_rec")
            cv0 = state_lru_conv[j]
            ypre, h_new = _lru_step(rec[0], gate[0], state_lru_h[j], cv0.transpose(1, 0, 2), *lru)
            xs = _mm(ypre[None], lru_w_out[j], tn=tn_out, resid=xs, gate=ms[2], name="lru_out")
            new_h_s.append(h_new)
            new_cv_s.append(jnp.concatenate([cv0[:, 1:, :], rec[0][:, None, :]], axis=1))
        else:
            xp, pb = _pool_seq(xp, norm_mix[i], mp[0], mp[1], mp[2], jnp.zeros((bp, POOL_BUF, d), F32),
                               pool_w[j], pool_scale[j], 0)
            new_pb_p.append(pb)
            buf0 = state_pool[j]
            xs2, u_s = _pool_step(xs[0], norm_mix[i], ms[0][0], ms[1][0], ms[2][0], buf0.transpose(1, 0, 2),
                                  pool_w[j], pool_scale[j], PAST_LEN)
            xs = xs2[None]
            new_pb_s.append(jnp.concatenate([buf0[:, 1:, :], u_s[:, None, :]], axis=1))
        xp, xs = _moe(i, xp, xs, (mp[3], mp[4], mp[5]), (ms[3], ms[4], ms[5]), norm_ffn[i],
                      router_w[i], router_bias[i], exp_w_gate, exp_w_up, exp_w_down,
                      shared_w_gate, shared_w_up, shared_w_down)
    y_prompt = _norm_out(xp, norm_out)
    y_sample = _norm_out(xs, norm_out).reshape(bs, 1, d)
    return (y_prompt, y_sample, jnp.stack(new_h_p), jnp.stack(new_cv_p), jnp.stack(new_pb_p),
            jnp.stack(new_h_s), jnp.stack(new_cv_s), jnp.stack(new_pb_s))
```
